```python
import math
import jax, jax.numpy as jnp
from jax import lax
import numpy as np

D_MODEL = 1024
BATCH = 4
SEQ = 4096
DEPTH = 2
DEC_BATCH = 32
DEC_SEQ = 4
PAST_LEN = 16384
PAGE_SIZE = 128

N_A_LAYERS = DEPTH // 2
N_B_LAYERS = DEPTH - N_A_LAYERS
POOL_WINDOWS = (2, 4, 8, 16)
POOL_GROUPS = len(POOL_WINDOWS)
POOL_GROUP_DIM = D_MODEL // POOL_GROUPS
POOL_PAD = max(POOL_WINDOWS) - 1
DIL_PATTERNS = ((128, 1), (512, 4), (2048, 16))
N_DIL_GROUPS = len(DIL_PATTERNS)
HEAD_DIM = 64
HEADS_PER_GROUP = D_MODEL // 128
N_HEADS = N_DIL_GROUPS * HEADS_PER_GROUP
ATTN_GROUP_WIDTH = HEADS_PER_GROUP * HEAD_DIM
Q_WIDTH = N_DIL_GROUPS * ATTN_GROUP_WIDTH
KV_WIDTH = 2 * Q_WIDTH
N_BUCKETS = 32
MAX_DISTANCE = 2048
D_FF = ((8 * D_MODEL // 3 + 127) // 128) * 128
ALPHA = (2.0 * DEPTH) ** 0.25
BETA = (8.0 * DEPTH) ** -0.25
LN_EPS = 1e-5
Q_BLOCK = 128
NEG = -1e30

kernel_name = 'yoco_pool_dilated_macaron_deepnorm_step'


def t5_buckets(dist):
    dist = np.asarray(dist, dtype=np.int64)
    max_exact = N_BUCKETS // 2
    ratio = np.maximum(dist, 1).astype(np.float64) / max_exact
    large = max_exact + (np.log(ratio) / np.log(MAX_DISTANCE / max_exact) * (N_BUCKETS - max_exact)).astype(np.int64)
    large = np.minimum(large, N_BUCKETS - 1)
    return np.where(dist < max_exact, dist, large).astype(np.int32)


def layer_norm(x, g, b):
    xf = x.astype(jnp.float32)
    mu = jnp.mean(xf, axis=-1, keepdims=True)
    var = jnp.mean(jnp.square(xf - mu), axis=-1, keepdims=True)
    y = (xf - mu) * lax.rsqrt(var + LN_EPS) * g.astype(jnp.float32) + b.astype(jnp.float32)
    return y.astype(x.dtype)


def swiglu(x, w13, w2):
    gate, up = jnp.split(x @ w13, 2, axis=-1)
    return (jax.nn.silu(gate) * up) @ w2


def pool_mixer(x, prefix, pos0, w_in, w_grp, scale, w_out):
    n = x.shape[1]
    u = x @ w_in
    u_ext = jnp.concatenate([prefix.astype(u.dtype), u], axis=1)
    csum = jnp.cumsum(u_ext.astype(jnp.float32), axis=1)
    csum = jnp.pad(csum, ((0, 0), (1, 0), (0, 0)))
    hi = csum[:, POOL_PAD + 1:POOL_PAD + 1 + n]
    pos = pos0 + jnp.arange(n)
    outs = []
    for g, w in enumerate(POOL_WINDOWS):
        cs = slice(g * POOL_GROUP_DIM, (g + 1) * POOL_GROUP_DIM)
        win_sum = hi[..., cs] - csum[:, POOL_PAD + 1 - w:POOL_PAD + 1 - w + n, cs]
        count = jnp.minimum(pos + 1, w).astype(jnp.float32)[None, :, None]
        pooled = win_sum / count - u[..., cs].astype(jnp.float32)
        outs.append(jnp.einsum('bnc,ce->bne', pooled.astype(u.dtype), w_grp[g]))
    z = jnp.concatenate(outs, axis=-1) * scale
    return z @ w_out, u_ext[:, -POOL_PAD:]


def dilated_group_attention(q, k, v, q_off, dilation, n_keys, bias):
    steps = jnp.arange(n_keys) * dilation

    def attend(q_blk, rows):
        idx = rows[:, None] - steps[None, :]
        valid = idx >= 0
        idx = jnp.maximum(idx, 0)
        kg = jnp.take(k, idx, axis=1, mode='clip')
        vg = jnp.take(v, idx, axis=1, mode='clip')
        s = jnp.einsum('bqhd,bqkhd->bhqk', q_blk, kg).astype(jnp.float32)
        s = s + bias.astype(jnp.float32)[None, :, None, :]
        s = jnp.where(valid[None, None], s, NEG)
        lse = jax.nn.logsumexp(s, axis=-1)
        p = jnp.exp(s - lse[..., None])
        o = jnp.einsum('bhqk,bqkhd->bqhd', p.astype(vg.dtype), vg)
        return o, jnp.transpose(lse, (0, 2, 1))

    b, nq, h, d = q.shape
    rows = q_off + jnp.arange(nq)
    if nq > Q_BLOCK and nq % Q_BLOCK == 0:
        nb = nq // Q_BLOCK
        qb = q.reshape(b, nb, Q_BLOCK, h, d).swapaxes(0, 1)
        rb = rows.reshape(nb, Q_BLOCK)
        o, lse = lax.map(lambda a: attend(a[0], a[1]), (qb, rb))
        o = o.swapaxes(0, 1).reshape(b, nq, h, d)
        lse = lse.swapaxes(0, 1).reshape(b, nq, h)
    else:
        o, lse = attend(q, rows)
    return o, lse


def dilated_mixer(x, kv_groups, q_offs, w_q, w_o, rel_bias):
    b, n, _ = x.shape
    q = (x @ w_q).reshape(b, n, N_DIL_GROUPS, HEADS_PER_GROUP, HEAD_DIM) * (HEAD_DIM ** -0.5)
    outs, lses = [], []
    for g, (win, dil) in enumerate(DIL_PATTERNS):
        n_keys = win // dil + 1
        buckets = t5_buckets(np.arange(n_keys) * dil)
        bias = rel_bias[buckets][:, g * HEADS_PER_GROUP:(g + 1) * HEADS_PER_GROUP].T
        k, v = kv_groups[g]
        o, lse = dilated_group_attention(q[:, :, g], k, v, q_offs[g], dil, n_keys, bias)
        outs.append(o)
        lses.append(lse)
    wts = jax.nn.softmax(jnp.stack(lses), axis=0)
    o = jnp.sum(jnp.stack(outs).astype(jnp.float32) * wts[..., None], axis=0).astype(x.dtype)
    return o.reshape(b, n, ATTN_GROUP_WIDTH) @ w_o


def trunk(x, pool_prefix, kv_past, pos0, ln_g, ln_b, ffn_w13, ffn_w2, pool_w_in, pool_w_grp, pool_scale,
          pool_w_out, attn_w_kv, attn_w_q, attn_w_o, rel_bias):
    pool_states = []
    kv_new = None
    kv_groups, q_offs = [], []
    for layer in range(DEPTH):
        if layer == N_A_LAYERS:
            b, n, _ = x.shape
            kv_new = (x @ attn_w_kv).reshape(b, n, N_DIL_GROUPS, 2, HEADS_PER_GROUP, HEAD_DIM)
            for g in range(N_DIL_GROUPS):
                if kv_past is None:
                    kv_g, off = kv_new[:, :, g], 0
                else:
                    kv_g = jnp.concatenate([kv_past[g].astype(kv_new.dtype), kv_new[:, :, g]], axis=1)
                    off = kv_past[g].shape[1]
                kv_groups.append((kv_g[:, :, 0], kv_g[:, :, 1]))
                q_offs.append(off)
        x = layer_norm(ALPHA * x + 0.5 * swiglu(x, ffn_w13[layer, 0], ffn_w2[layer, 0]), ln_g[layer, 0], ln_b[layer, 0])
        if layer < N_A_LAYERS:
            mix, st = pool_mixer(x, pool_prefix[layer], pos0, pool_w_in[layer], pool_w_grp[layer],
                                 pool_scale[layer], pool_w_out[layer])
            pool_states.append(st)
        else:
            j = layer - N_A_LAYERS
            mix = dilated_mixer(x, kv_groups, q_offs, attn_w_q[j], attn_w_o[j], rel_bias)
        x = layer_norm(ALPHA * x + mix, ln_g[layer, 1], ln_b[layer, 1])
        x = layer_norm(ALPHA * x + 0.5 * swiglu(x, ffn_w13[layer, 1], ffn_w2[layer, 1]), ln_g[layer, 2], ln_b[layer, 2])
    return x, jnp.stack(pool_states), kv_new


def setup_inputs(seed: int = 0) -> dict:
    key = jax.random.key(seed)
    ks = jax.random.split(key, 20)

    def nrm(k, shape, scale):
        return jax.random.normal(k, shape, jnp.float32) * scale

    x_prompt = nrm(ks[0], (BATCH, SEQ, D_MODEL), 1.0)
    x_sample = nrm(ks[1], (DEC_BATCH, DEC_SEQ, D_MODEL), 1.0)
    state_pool = nrm(ks[2], (N_A_LAYERS, DEC_BATCH, POOL_PAD, D_MODEL), 1.0)
    caches = [nrm(ks[3 + g], (DEC_BATCH, min(w, PAST_LEN), 2, HEADS_PER_GROUP, HEAD_DIM), 1.0)
              for g, (w, _) in enumerate(DIL_PATTERNS)]
    ln_g = 1.0 + nrm(ks[6], (DEPTH, 3, D_MODEL), 0.05)
    ln_b = nrm(ks[7], (DEPTH, 3, D_MODEL), 0.02)
    ffn_w13 = nrm(ks[8], (DEPTH, 2, D_MODEL, 2 * D_FF), D_MODEL ** -0.5)
    ffn_w2 = nrm(ks[9], (DEPTH, 2, D_FF, D_MODEL), (D_FF ** -0.5) * BETA)
    pool_w_in = nrm(ks[10], (N_A_LAYERS, D_MODEL, D_MODEL), D_MODEL ** -0.5)
    pool_w_grp = nrm(ks[11], (N_A_LAYERS, POOL_GROUPS, POOL_GROUP_DIM, POOL_GROUP_DIM), POOL_GROUP_DIM ** -0.5)
    pool_scale = 1.0 + nrm(ks[12], (N_A_LAYERS, D_MODEL), 0.1)
    pool_w_out = nrm(ks[13], (N_A_LAYERS, D_MODEL, D_MODEL), (D_MODEL ** -0.5) * BETA)
    kv_scale = jnp.array([1.0, BETA], jnp.float32)[None, None, :, None]
    attn_w_kv = (nrm(ks[14], (D_MODEL, N_DIL_GROUPS, 2, ATTN_GROUP_WIDTH), D_MODEL ** -0.5) * kv_scale).reshape(D_MODEL, KV_WIDTH)
    attn_w_q = nrm(ks[15], (N_B_LAYERS, D_MODEL, Q_WIDTH), D_MODEL ** -0.5)
    attn_w_o = nrm(ks[16], (N_B_LAYERS, ATTN_GROUP_WIDTH, D_MODEL), (ATTN_GROUP_WIDTH ** -0.5) * BETA)
    rel_bias = nrm(ks[17], (N_BUCKETS, N_HEADS), 0.3)
    return {'x_prompt': x_prompt, 'x_sample': x_sample, 'state_pool': state_pool,
            'cache_kv_w128': caches[0], 'cache_kv_w512': caches[1], 'cache_kv_w2048': caches[2],
            'ln_g': ln_g, 'ln_b': ln_b, 'ffn_w13': ffn_w13, 'ffn_w2': ffn_w2,
            'pool_w_in': pool_w_in, 'pool_w_grp': pool_w_grp, 'pool_scale': pool_scale, 'pool_w_out': pool_w_out,
            'attn_w_kv': attn_w_kv, 'attn_w_q': attn_w_q, 'attn_w_o': attn_w_o, 'rel_bias': rel_bias}


def reference(x_prompt, x_sample, state_pool, cache_kv_w128, cache_kv_w512, cache_kv_w2048, ln_g, ln_b, ffn_w13,
              ffn_w2, pool_w_in, pool_w_grp, pool_scale, pool_w_out, attn_w_kv, attn_w_q, attn_w_o, rel_bias):
    weights = (ln_g, ln_b, ffn_w13, ffn_w2, pool_w_in, pool_w_grp, pool_scale, pool_w_out,
               attn_w_kv, attn_w_q, attn_w_o, rel_bias)
    pool_zero = jnp.zeros((N_A_LAYERS, x_prompt.shape[0], POOL_PAD, D_MODEL), x_prompt.dtype)
    y_prompt, pool_p, kv_p = trunk(x_prompt, pool_zero, None, 0, *weights)
    y_sample, pool_s, kv_s = trunk(x_sample, state_pool, (cache_kv_w128, cache_kv_w512, cache_kv_w2048),
                                   PAST_LEN, *weights)
    n = x_prompt.shape[1]
    kv_p_groups = [kv_p[:, n - min(w, n):, g] for g, (w, _) in enumerate(DIL_PATTERNS)]
    return (y_prompt, y_sample, pool_p, pool_s, kv_p_groups[0], kv_p_groups[1], kv_p_groups[2],
            kv_s[:, :, 0], kv_s[:, :, 1], kv_s[:, :, 2])
```

```python
import functools

import numpy as np
import jax
import jax.numpy as jnp
from jax import lax
from jax.experimental import pallas as pl
from jax.experimental.pallas import tpu as pltpu

F32 = jnp.float32
BF16 = jnp.bfloat16

D_MODEL = 1024
DEPTH = 2
PAST_LEN = 16384
POOL_WINDOWS = (2, 4, 8, 16)
POOL_GROUP_DIM = D_MODEL // len(POOL_WINDOWS)
POOL_PAD = max(POOL_WINDOWS) - 1
DIL_PATTERNS = ((128, 1), (512, 4), (2048, 16))
N_DIL_GROUPS = len(DIL_PATTERNS)
HEAD_DIM = 64
HEADS_PER_GROUP = 8
ATTN_GROUP_WIDTH = HEADS_PER_GROUP * HEAD_DIM
Q_WIDTH = N_DIL_GROUPS * ATTN_GROUP_WIDTH
KV_WIDTH = 2 * Q_WIDTH
N_BUCKETS = 32
MAX_DISTANCE = 2048
D_FF = 2816
ALPHA = (2.0 * DEPTH) ** 0.25
LN_EPS = 1e-5
NEG = -1e30
N_KEYS = 129

LANES = 128
MXU_DIM = 256
VMEM_LIMIT_BYTES = 56 * 1024 * 1024

FF_CHUNK = MXU_DIM
ROW_TILE = 512
Q_BLOCK = 128
ATTN_CHUNK = 2048


def _bucket_table(dist):
    dist = np.asarray(dist, dtype=np.int64)
    max_exact = N_BUCKETS // 2
    ratio = np.maximum(dist, 1).astype(np.float64) / max_exact
    large = max_exact + (np.log(ratio) / np.log(MAX_DISTANCE / max_exact) * (N_BUCKETS - max_exact)).astype(np.int64)
    large = np.minimum(large, N_BUCKETS - 1)
    return np.where(dist < max_exact, dist, large).astype(np.int32)


def _layer_norm(z, g, b):
    mu = jnp.mean(z, axis=-1, keepdims=True)
    zc = z - mu
    var = jnp.mean(zc * zc, axis=-1, keepdims=True)
    return zc * lax.rsqrt(var + LN_EPS) * g + b


def _dot(a, b):
    return jnp.dot(a, b, preferred_element_type=F32)


def _swiglu(xb, w13_ref, w2_ref):
    acc = None
    for j in range(D_FF // FF_CHUNK):
        lo = j * FF_CHUNK
        gate = _dot(xb, w13_ref[:, lo:lo + FF_CHUNK])
        up = _dot(xb, w13_ref[:, D_FF + lo:D_FF + lo + FF_CHUNK])
        act = (gate * jax.nn.sigmoid(gate) * up).astype(BF16)
        y = _dot(act, w2_ref[lo:lo + FF_CHUNK, :])
        acc = y if acc is None else acc + y
    return acc


def _ffn_kernel(x_ref, w13_ref, w2_ref, g_ref, b_ref, o_ref):
    x = x_ref[...]
    y = _swiglu(x.astype(BF16), w13_ref, w2_ref)
    o_ref[...] = _layer_norm(ALPHA * x + 0.5 * y, g_ref[...], b_ref[...])


def _ffn_proj_kernel(x_ref, w13_ref, w2_ref, g_ref, b_ref, wp_ref, o_ref, p_ref, *, proj_scale):
    x = x_ref[...]
    y = _swiglu(x.astype(BF16), w13_ref, w2_ref)
    out = _layer_norm(ALPHA * x + 0.5 * y, g_ref[...], b_ref[...])
    o_ref[...] = out
    p_ref[...] = _dot(out.astype(BF16), wp_ref[...]) * proj_scale


def _resident(shape):
    nd = len(shape)
    return pl.BlockSpec(shape, lambda *_: (0,) * nd, pipeline_mode=pl.Buffered(1))


def _row_tile(rows):
    return ROW_TILE if rows % ROW_TILE == 0 else rows


def _ffn(x, w13, w2, g, b, wp=None, proj_scale=1.0):
    rows = x.shape[0]
    tm = _row_tile(rows)
    row_spec = pl.BlockSpec((tm, D_MODEL), lambda i: (i, 0))
    in_specs = [row_spec, _resident(w13.shape), _resident(w2.shape), _resident(g.shape), _resident(b.shape)]
    params = pltpu.CompilerParams(dimension_semantics=("arbitrary",), vmem_limit_bytes=VMEM_LIMIT_BYTES)
    if wp is None:
        return pl.pallas_call(
            _ffn_kernel, grid=(rows // tm,), in_specs=in_specs, out_specs=row_spec,
            out_shape=jax.ShapeDtypeStruct((rows, D_MODEL), F32), compiler_params=params,
            name="ffn")(x, w13, w2, g, b)
    pw = wp.shape[1]
    return pl.pallas_call(
        functools.partial(_ffn_proj_kernel, proj_scale=proj_scale), grid=(rows // tm,),
        in_specs=in_specs + [_resident(wp.shape)],
        out_specs=[row_spec, pl.BlockSpec((tm, pw), lambda i: (i, 0))],
        out_shape=[jax.ShapeDtypeStruct((rows, D_MODEL), F32), jax.ShapeDtypeStruct((rows, pw), F32)],
        compiler_params=params, name="ffn_proj")(x, w13, w2, g, b, wp)


def _oproj_kernel(x_ref, a_ref, wo_ref, g_ref, b_ref, o_ref):
    mix = _dot(a_ref[...].astype(BF16), wo_ref[...])
    o_ref[...] = _layer_norm(ALPHA * x_ref[...] + mix, g_ref[...], b_ref[...])


def _oproj(x, a, wo, g, b):
    rows = x.shape[0]
    tm = _row_tile(rows)
    row_spec = pl.BlockSpec((tm, D_MODEL), lambda i: (i, 0))
    return pl.pallas_call(
        _oproj_kernel, grid=(rows // tm,),
        in_specs=[row_spec, pl.BlockSpec((tm, ATTN_GROUP_WIDTH), lambda i: (i, 0)),
                  _resident(wo.shape), _resident(g.shape), _resident(b.shape)],
        out_specs=row_spec, out_shape=jax.ShapeDtypeStruct((rows, D_MODEL), F32),
        compiler_params=pltpu.CompilerParams(dimension_semantics=("arbitrary",),
                                             vmem_limit_bytes=VMEM_LIMIT_BYTES),
        name="oproj")(x, a, wo, g, b)


CARRY_ROWS = 16


def _pool_tail(x, u, pooled_groups, wgrp_ref, scale_ref, wout_ref, g_ref, b_ref):
    zs = [_dot(p.astype(BF16), wgrp_ref[gi]) for gi, p in enumerate(pooled_groups)]
    z = jnp.concatenate(zs, axis=-1) * scale_ref[...]
    mix = _dot(z.astype(BF16), wout_ref[...])
    return _layer_norm(ALPHA * x + mix, g_ref[...], b_ref[...])


def _pool_prompt_kernel(x_ref, win_ref, wgrp_ref, scale_ref, wout_ref, g_ref, b_ref, o_ref, st_ref, ue_ref,
                        *, tm, tiles):
    t = pl.program_id(1)

    @pl.when(t == 0)
    def _():
        ue_ref[0:CARRY_ROWS, :] = jnp.zeros((CARRY_ROWS, D_MODEL), F32)

    x = x_ref[...]
    u = _dot(x.astype(BF16), win_ref[...])
    ue_ref[CARRY_ROWS:CARRY_ROWS + tm, :] = u
    pos = t * tm + lax.broadcasted_iota(jnp.int32, (tm, 1), 0)
    pooled = []
    for gi, w in enumerate(POOL_WINDOWS):
        cols = slice(gi * POOL_GROUP_DIM, (gi + 1) * POOL_GROUP_DIM)
        ug = u[:, cols]
        win_sum = ug
        for k in range(1, w):
            win_sum = win_sum + ue_ref[CARRY_ROWS - k:CARRY_ROWS - k + tm, cols]
        count = jnp.minimum(pos + 1, w).astype(F32)
        pooled.append(win_sum / count - ug)
    o_ref[...] = _pool_tail(x, u, pooled, wgrp_ref, scale_ref, wout_ref, g_ref, b_ref)

    tail = ue_ref[tm:tm + CARRY_ROWS, :]

    @pl.when(t == tiles - 1)
    def _():
        st_ref[0] = tail

    ue_ref[0:CARRY_ROWS, :] = tail


def _pool_prompt(x, batch, w_in, w_grp, scale, w_out, g, b):
    rows = x.shape[0]
    seq = rows // batch
    tm = _row_tile(seq)
    tiles = seq // tm
    row_spec = pl.BlockSpec((tm, D_MODEL), lambda bi, t: (bi * tiles + t, 0))
    return pl.pallas_call(
        functools.partial(_pool_prompt_kernel, tm=tm, tiles=tiles), grid=(batch, tiles),
        in_specs=[row_spec, _resident(w_in.shape), _resident(w_grp.shape), _resident(scale.shape),
                  _resident(w_out.shape), _resident(g.shape), _resident(b.shape)],
        out_specs=[row_spec, pl.BlockSpec((1, CARRY_ROWS, D_MODEL), lambda bi, t: (bi, 0, 0))],
        out_shape=[jax.ShapeDtypeStruct((rows, D_MODEL), F32),
                   jax.ShapeDtypeStruct((batch, CARRY_ROWS, D_MODEL), F32)],
        scratch_shapes=[pltpu.VMEM((tm + CARRY_ROWS, D_MODEL), F32)],
        compiler_params=pltpu.CompilerParams(dimension_semantics=("arbitrary", "arbitrary"),
                                             vmem_limit_bytes=VMEM_LIMIT_BYTES),
        name="pool_prompt")(x, w_in, w_grp, scale, w_out, g, b)


def _pool_sample_kernel(x_ref, st_ref, win_ref, wgrp_ref, scale_ref, wout_ref, g_ref, b_ref, o_ref, nst_ref,
                        *, steps, batch):
    x = x_ref[...]
    u = _dot(x.astype(BF16), win_ref[...])
    ext = [st_ref[k] for k in range(POOL_PAD)] + [u[i * batch:(i + 1) * batch, :] for i in range(steps)]
    pooled = []
    for gi, w in enumerate(POOL_WINDOWS):
        cols = slice(gi * POOL_GROUP_DIM, (gi + 1) * POOL_GROUP_DIM)
        per_step = []
        for i in range(steps):
            win_sum = ext[POOL_PAD + i][:, cols]
            for k in range(1, w):
                win_sum = win_sum + ext[POOL_PAD + i - k][:, cols]
            count = float(min(PAST_LEN + i + 1, w))
            per_step.append(win_sum / count - ext[POOL_PAD + i][:, cols])
        pooled.append(jnp.concatenate(per_step, axis=0))
    o_ref[...] = _pool_tail(x, u, pooled, wgrp_ref, scale_ref, wout_ref, g_ref, b_ref)
    for k in range(POOL_PAD):
        nst_ref[k] = ext[steps + k]


def _pool_sample(x, st, steps, batch, w_in, w_grp, scale, w_out, g, b):
    rows = x.shape[0]
    full = lambda shape: pl.BlockSpec(shape, lambda i: (0,) * len(shape))
    return pl.pallas_call(
        functools.partial(_pool_sample_kernel, steps=steps, batch=batch), grid=(1,),
        in_specs=[full(x.shape), full(st.shape), full(w_in.shape), full(w_grp.shape), full(scale.shape),
                  full(w_out.shape), full(g.shape), full(b.shape)],
        out_specs=[full(x.shape), full(st.shape)],
        out_shape=[jax.ShapeDtypeStruct((rows, D_MODEL), F32), jax.ShapeDtypeStruct(st.shape, F32)],
        compiler_params=pltpu.CompilerParams(dimension_semantics=("arbitrary",),
                                             vmem_limit_bytes=VMEM_LIMIT_BYTES),
        name="pool_sample")(x, st, w_in, w_grp, scale, w_out, g, b)


def _attn_prompt_kernel(q0_ref, q1_ref, q2_ref, k0_ref, k1_ref, k2_ref, v0_ref, v1_ref, v2_ref, bias_ref,
                        o_ref, kbuf, vbuf, oslab, lslab):
    c = pl.program_id(2)
    slot = c % 2
    cur_base = pl.multiple_of(slot * ATTN_CHUNK, ATTN_CHUNK)
    prev_base = pl.multiple_of((1 - slot) * ATTN_CHUNK, ATTN_CHUNK)
    q_refs = (q0_ref, q1_ref, q2_ref)
    k_refs = (k0_ref, k1_ref, k2_ref)
    v_refs = (v0_ref, v1_ref, v2_ref)

    @pl.when(c == 0)
    def _():
        zeros = jnp.zeros((ATTN_CHUNK, LANES), F32)
        for g in range(N_DIL_GROUPS):
            kbuf[g, pl.ds(prev_base, ATTN_CHUNK), :] = zeros
            vbuf[g, pl.ds(prev_base, ATTN_CHUNK), :] = zeros

    for g in range(N_DIL_GROUPS):
        kbuf[g, pl.ds(cur_base, ATTN_CHUNK), :] = k_refs[g][0]
        vbuf[g, pl.ds(cur_base, ATTN_CHUNK), :] = v_refs[g][0]

    lane = lax.broadcasted_iota(jnp.int32, (1, LANES), 1)
    first_head = lane < HEAD_DIM
    key_col = lax.broadcasted_iota(jnp.int32, (1, 2 * Q_BLOCK), 1)

    def strided(start, dil):
        return pl.ds(start, Q_BLOCK) if dil == 1 else pl.ds(start, Q_BLOCK, stride=dil)

    for g, (_, dil) in enumerate(DIL_PATTERNS):
        sub_blocks = ATTN_CHUNK // (Q_BLOCK * dil)
        span = Q_BLOCK * dil

        def block(blk, carry, g=g, dil=dil, sub_blocks=sub_blocks, span=span):
            res = blk // sub_blocks
            sb = blk % sub_blocks
            start = res + sb * span
            prev_start = jnp.where(sb > 0, cur_base + start - span,
                                   prev_base + res + (sub_blocks - 1) * span)
            qb = q_refs[g][0, strided(start, dil), :].astype(BF16)
            kk = jnp.concatenate([kbuf[g, strided(prev_start, dil), :],
                                  kbuf[g, strided(cur_base + start, dil), :]], axis=0).astype(BF16)
            vv = jnp.concatenate([vbuf[g, strided(prev_start, dil), :],
                                  vbuf[g, strided(cur_base + start, dil), :]], axis=0).astype(BF16)
            no_prev = jnp.logical_and(jnp.logical_and(c == 0, sb == 0), key_col < Q_BLOCK)
            outs, lses = [], []
            for hh in range(2):
                head_lanes = first_head if hh == 0 else jnp.logical_not(first_head)
                qh = jnp.where(head_lanes, qb, jnp.zeros_like(qb))
                s = lax.dot_general(qh, kk, (((1,), (1,)), ((), ())), preferred_element_type=F32)
                s = s + bias_ref[g, hh]
                s = jnp.where(no_prev, NEG, s)
                m = jnp.max(s, axis=1, keepdims=True)
                p = jnp.exp(s - m)
                l = jnp.sum(p, axis=1, keepdims=True)
                pv = _dot(p.astype(BF16), vv)
                outs.append(pv / l)
                lses.append(m + jnp.log(l))
            oslab[g, strided(start, dil), :] = jnp.where(first_head, outs[0], outs[1])
            lslab[g, strided(start, dil), :] = jnp.where(first_head, lses[0], lses[1])
            return carry

        lax.fori_loop(0, ATTN_CHUNK // Q_BLOCK, block, 0)

    def combine(i, carry):
        rows = pl.ds(pl.multiple_of(i * MXU_DIM, MXU_DIM), MXU_DIM)
        ls = [lslab[g, rows, :] for g in range(N_DIL_GROUPS)]
        top = jnp.maximum(jnp.maximum(ls[0], ls[1]), ls[2])
        es = [jnp.exp(l - top) for l in ls]
        den = es[0] + es[1] + es[2]
        num = es[0] * oslab[0, rows, :] + es[1] * oslab[1, rows, :] + es[2] * oslab[2, rows, :]
        o_ref[0, rows, :] = num / den
        return carry

    lax.fori_loop(0, ATTN_CHUNK // MXU_DIM, combine, 0)


def _prompt_bias_tiles(rel_bias):
    qi = np.arange(Q_BLOCK)[:, None]
    kj = np.arange(2 * Q_BLOCK)[None, :]
    dist = Q_BLOCK + qi - kj
    valid = (dist >= 0) & (dist < N_KEYS)
    dist = np.clip(dist, 0, N_KEYS - 1)
    tiles = []
    for g, (_, dil) in enumerate(DIL_PATTERNS):
        buckets = _bucket_table(np.arange(N_KEYS) * dil)
        bias = rel_bias[buckets][:, g * HEADS_PER_GROUP:(g + 1) * HEADS_PER_GROUP].T
        tiles.append(jnp.where(valid[None], bias[:, dist], NEG))
    return jnp.stack(tiles).astype(F32)


def _attn_prompt(q, kv, bias_tiles):
    batch, seq, _ = q.shape
    chunks = seq // ATTN_CHUNK
    pairs = ATTN_GROUP_WIDTH // LANES
    blk = (1, ATTN_CHUNK, LANES)
    q_specs = [pl.BlockSpec(blk, functools.partial(lambda b, hp, c, g: (b, c, g * pairs + hp), g=g))
               for g in range(N_DIL_GROUPS)]
    k_specs = [pl.BlockSpec(blk, functools.partial(lambda b, hp, c, g: (b, c, 2 * g * pairs + hp), g=g))
               for g in range(N_DIL_GROUPS)]
    v_specs = [pl.BlockSpec(blk, functools.partial(lambda b, hp, c, g: (b, c, (2 * g + 1) * pairs + hp), g=g))
               for g in range(N_DIL_GROUPS)]
    bias_spec = pl.BlockSpec((N_DIL_GROUPS, 2, Q_BLOCK, 2 * Q_BLOCK), lambda b, hp, c: (0, hp, 0, 0))
    return pl.pallas_call(
        _attn_prompt_kernel, grid=(batch, pairs, chunks),
        in_specs=q_specs + k_specs + v_specs + [bias_spec],
        out_specs=pl.BlockSpec(blk, lambda b, hp, c: (b, c, hp)),
        out_shape=jax.ShapeDtypeStruct((batch, seq, ATTN_GROUP_WIDTH), F32),
        scratch_shapes=[pltpu.VMEM((N_DIL_GROUPS, 2 * ATTN_CHUNK, LANES), F32),
                        pltpu.VMEM((N_DIL_GROUPS, 2 * ATTN_CHUNK, LANES), F32),
                        pltpu.VMEM((N_DIL_GROUPS, ATTN_CHUNK, LANES), F32),
                        pltpu.VMEM((N_DIL_GROUPS, ATTN_CHUNK, LANES), F32)],
        compiler_params=pltpu.CompilerParams(dimension_semantics=("arbitrary", "arbitrary", "arbitrary"),
                                             vmem_limit_bytes=VMEM_LIMIT_BYTES),
        name="attn_prompt")(q, q, q, kv, kv, kv, kv, kv, kv, bias_tiles)


def _attn_sample_kernel(q_ref, kvn_ref, c0_ref, c1_ref, c2_ref, bias_ref, o_ref, *, steps):
    for i in range(steps):
        outs, lses = [], []
        for g in range(N_DIL_GROUPS):
            if g == 0:
                ks = [c0_ref[0, i:, 0], kvn_ref[0, 0:i + 1, 0, 0]]
                vs = [c0_ref[0, i:, 1], kvn_ref[0, 0:i + 1, 0, 1]]
            else:
                c_ref = c1_ref if g == 1 else c2_ref
                ks = [c_ref[0, :, i, 0], kvn_ref[0, i:i + 1, g, 0]]
                vs = [c_ref[0, :, i, 1], kvn_ref[0, i:i + 1, g, 1]]
            kk = jnp.concatenate(ks, axis=0)
            vv = jnp.concatenate(vs, axis=0)
            qv = q_ref[0, i, g]
            s = jnp.sum(kk * qv[None], axis=-1, keepdims=True) + bias_ref[g]
            m = jnp.max(s, axis=0, keepdims=True)
            p = jnp.exp(s - m)
            l = jnp.sum(p, axis=0, keepdims=True)
            outs.append(jnp.sum(p * vv, axis=0) / l[0])
            lses.append(m[0] + jnp.log(l[0]))
        top = jnp.maximum(jnp.maximum(lses[0], lses[1]), lses[2])
        es = [jnp.exp(l - top) for l in lses]
        den = es[0] + es[1] + es[2]
        o_ref[0, i] = (es[0] * outs[0] + es[1] * outs[1] + es[2] * outs[2]) / den


def _sample_bias(rel_bias):
    rows = []
    for g, (_, dil) in enumerate(DIL_PATTERNS):
        buckets = _bucket_table(np.arange(N_KEYS) * dil)[::-1].copy()
        rows.append(rel_bias[buckets][:, g * HEADS_PER_GROUP:(g + 1) * HEADS_PER_GROUP])
    return jnp.stack(rows)[..., None].astype(F32)


def _attn_sample(q, kvn, caches, bias):
    batch, steps = q.shape[:2]
    tail = (2, HEADS_PER_GROUP, HEAD_DIM)
    views, specs = [caches[0]], [pl.BlockSpec((1, DIL_PATTERNS[0][0]) + tail, lambda b: (b, 0, 0, 0, 0))]
    for g in (1, 2):
        win, dil = DIL_PATTERNS[g]
        assert steps <= dil and caches[g].shape[1] == win
        views.append(caches[g].reshape(batch, win // dil, dil, *tail))
        specs.append(pl.BlockSpec((1, win // dil, steps) + tail, lambda b: (b, 0, 0, 0, 0, 0)))
    assert caches[0].shape[1] == DIL_PATTERNS[0][0]
    return pl.pallas_call(
        functools.partial(_attn_sample_kernel, steps=steps), grid=(batch,),
        in_specs=[pl.BlockSpec((1,) + q.shape[1:], lambda b: (b, 0, 0, 0, 0)),
                  pl.BlockSpec((1,) + kvn.shape[1:], lambda b: (b, 0, 0, 0, 0, 0))] + specs
                 + [pl.BlockSpec(bias.shape, lambda b: (0, 0, 0, 0))],
        out_specs=pl.BlockSpec((1, steps, HEADS_PER_GROUP, HEAD_DIM), lambda b: (b, 0, 0, 0)),
        out_shape=jax.ShapeDtypeStruct((batch, steps, HEADS_PER_GROUP, HEAD_DIM), F32),
        compiler_params=pltpu.CompilerParams(dimension_semantics=("arbitrary",),
                                             vmem_limit_bytes=VMEM_LIMIT_BYTES),
        name="attn_sample")(q, kvn, *views, bias)


def kernel(x_prompt, x_sample, state_pool, cache_kv_w128, cache_kv_w512, cache_kv_w2048, ln_g, ln_b, ffn_w13,
           ffn_w2, pool_w_in, pool_w_grp, pool_scale, pool_w_out, attn_w_kv, attn_w_q, attn_w_o, rel_bias):
    batch, seq, _ = x_prompt.shape
    dec_batch, steps, _ = x_sample.shape
    assert DEPTH == 2 and state_pool.shape[0] == 1 and attn_w_q.shape[0] == 1
    assert seq % ATTN_CHUNK == 0 and seq >= DIL_PATTERNS[-1][0]

    w13 = ffn_w13.astype(BF16)
    w2 = ffn_w2.astype(BF16)
    w_in = pool_w_in[0].astype(BF16)
    w_grp = pool_w_grp[0].astype(BF16)
    w_out = pool_w_out[0].astype(BF16)
    w_kv = attn_w_kv.astype(BF16)
    w_q = attn_w_q[0].astype(BF16)
    w_o = attn_w_o[0].astype(BF16)
    scale = pool_scale[0][None]
    ln = lambda layer, k: (ln_g[layer, k][None], ln_b[layer, k][None])
    q_scale = HEAD_DIM ** -0.5

    def layer0_ffn_a(x):
        return _ffn(x, w13[0, 0], w2[0, 0], *ln(0, 0))

    def layer0_ffn_b(x):
        return _ffn(x, w13[0, 1], w2[0, 1], *ln(0, 2), wp=w_kv)

    def layer1_ffn_a(x):
        return _ffn(x, w13[1, 0], w2[1, 0], *ln(1, 0), wp=w_q, proj_scale=q_scale)

    def layer1_tail(x, attn):
        x = _oproj(x, attn, w_o, *ln(1, 1))
        return _ffn(x, w13[1, 1], w2[1, 1], *ln(1, 2))

    xp = x_prompt.reshape(batch * seq, D_MODEL)
    xp = layer0_ffn_a(xp)
    xp, pool_p = _pool_prompt(xp, batch, w_in, w_grp, scale, w_out, *ln(0, 1))
    xp, kv_p = layer0_ffn_b(xp)
    xp, q_p = layer1_ffn_a(xp)
    kv_p = kv_p.reshape(batch, seq, KV_WIDTH)
    attn_p = _attn_prompt(q_p.reshape(batch, seq, Q_WIDTH), kv_p, _prompt_bias_tiles(rel_bias))
    y_prompt = layer1_tail(xp, attn_p.reshape(batch * seq, ATTN_GROUP_WIDTH)).reshape(batch, seq, D_MODEL)
    pool_p = pool_p[:, CARRY_ROWS - POOL_PAD:][None]
    kv_p = kv_p.reshape(batch, seq, N_DIL_GROUPS, 2, HEADS_PER_GROUP, HEAD_DIM)
    kv_p_groups = [kv_p[:, seq - min(w, seq):, g] for g, (w, _) in enumerate(DIL_PATTERNS)]

    xs = jnp.swapaxes(x_sample, 0, 1).reshape(steps * dec_batch, D_MODEL)
    xs = layer0_ffn_a(xs)
    xs, pool_s = _pool_sample(xs, jnp.swapaxes(state_pool[0], 0, 1), steps, dec_batch,
                              w_in, w_grp, scale, w_out, *ln(0, 1))
    xs, kv_s = layer0_ffn_b(xs)
    xs, q_s = layer1_ffn_a(xs)
    kv_s = jnp.swapaxes(kv_s.reshape(steps, dec_batch, N_DIL_GROUPS, 2, HEADS_PER_GROUP, HEAD_DIM), 0, 1)
    q_s = jnp.swapaxes(q_s.reshape(steps, dec_batch, N_DIL_GROUPS, HEADS_PER_GROUP, HEAD_DIM), 0, 1)
    attn_s = _attn_sample(q_s, kv_s, (cache_kv_w128, cache_kv_w512, cache_kv_w2048), _sample_bias(rel_bias))
    attn_s = jnp.swapaxes(attn_s, 0, 1).reshape(steps * dec_batch, ATTN_GROUP_WIDTH)
    y_sample = jnp.swapaxes(layer1_tail(xs, attn_s).reshape(steps, dec_batch, D_MODEL), 0, 1)
    pool_s = jnp.swapaxes(pool_s, 0, 1)[None]

    return (y_prompt, y_sample, pool_p, pool_s, kv_p_groups[0], kv_p_groups[1], kv_p_groups[2],
            kv_s[:, :, 0], kv_s[:, :, 1], kv_s[:, :, 2])
```

```python
import functools

import numpy as np
import jax
import jax.numpy as jnp
from jax import lax
from jax.experimental import pallas as pl
from jax.experimental.pallas import tpu as pltpu

F32 = jnp.float32
BF16 = jnp.bfloat16

D_MODEL = 1024
DEPTH = 2
PAST_LEN = 16384
POOL_WINDOWS = (2, 4, 8, 16)
POOL_GROUP_DIM = D_MODEL // len(POOL_WINDOWS)
POOL_PAD = max(POOL_WINDOWS) - 1
DIL_PATTERNS = ((128, 1), (512, 4), (2048, 16))
N_DIL_GROUPS = len(DIL_PATTERNS)
HEAD_DIM = 64
HEADS_PER_GROUP = 8
ATTN_GROUP_WIDTH = HEADS_PER_GROUP * HEAD_DIM
Q_WIDTH = N_DIL_GROUPS * ATTN_GROUP_WIDTH
KV_WIDTH = 2 * Q_WIDTH
N_BUCKETS = 32
MAX_DISTANCE = 2048
D_FF = 2816
ALPHA = (2.0 * DEPTH) ** 0.25
LN_EPS = 1e-5
NEG = -1e30
N_KEYS = 129

LANES = 128
MXU_DIM = 256
VMEM_LIMIT_BYTES = 56 * 1024 * 1024

FF_CHUNK = MXU_DIM
ROW_TILE = 512
Q_BLOCK = 128
ATTN_CHUNK = 2048
BLOCK_UNROLL = 4


def _bucket_table(dist):
    dist = np.asarray(dist, dtype=np.int64)
    max_exact = N_BUCKETS // 2
    ratio = np.maximum(dist, 1).astype(np.float64) / max_exact
    large = max_exact + (np.log(ratio) / np.log(MAX_DISTANCE / max_exact) * (N_BUCKETS - max_exact)).astype(np.int64)
    large = np.minimum(large, N_BUCKETS - 1)
    return np.where(dist < max_exact, dist, large).astype(np.int32)


def _layer_norm(z, g, b):
    mu = jnp.mean(z, axis=-1, keepdims=True)
    zc = z - mu
    var = jnp.mean(zc * zc, axis=-1, keepdims=True)
    return zc * lax.rsqrt(var + LN_EPS) * g + b


def _dot(a, b):
    return jnp.dot(a, b, preferred_element_type=F32)


def _swiglu(xb, w13_ref, w2_ref):
    acc = None
    for j in range(D_FF // FF_CHUNK):
        lo = j * FF_CHUNK
        gate = _dot(xb, w13_ref[:, lo:lo + FF_CHUNK])
        up = _dot(xb, w13_ref[:, D_FF + lo:D_FF + lo + FF_CHUNK])
        act = (gate * jax.nn.sigmoid(gate) * up).astype(BF16)
        y = _dot(act, w2_ref[lo:lo + FF_CHUNK, :])
        acc = y if acc is None else acc + y
    return acc


def _ffn_kernel(x_ref, w13_ref, w2_ref, g_ref, b_ref, o_ref):
    x = x_ref[...]
    y = _swiglu(x.astype(BF16), w13_ref, w2_ref)
    o_ref[...] = _layer_norm(ALPHA * x + 0.5 * y, g_ref[...], b_ref[...])


def _ffn_proj_kernel(x_ref, w13_ref, w2_ref, g_ref, b_ref, wp_ref, o_ref, p_ref, *, proj_scale):
    x = x_ref[...]
    y = _swiglu(x.astype(BF16), w13_ref, w2_ref)
    out = _layer_norm(ALPHA * x + 0.5 * y, g_ref[...], b_ref[...])
    o_ref[...] = out
    p_ref[...] = _dot(out.astype(BF16), wp_ref[...]) * proj_scale


def _resident(shape):
    nd = len(shape)
    return pl.BlockSpec(shape, lambda *_: (0,) * nd, pipeline_mode=pl.Buffered(1))


def _row_tile(rows):
    return ROW_TILE if rows % ROW_TILE == 0 else rows


def _ffn(x, w13, w2, g, b, wp=None, proj_scale=1.0):
    rows = x.shape[0]
    tm = _row_tile(rows)
    row_spec = pl.BlockSpec((tm, D_MODEL), lambda i: (i, 0))
    in_specs = [row_spec, _resident(w13.shape), _resident(w2.shape), _resident(g.shape), _resident(b.shape)]
    params = pltpu.CompilerParams(dimension_semantics=("arbitrary",), vmem_limit_bytes=VMEM_LIMIT_BYTES)
    if wp is None:
        return pl.pallas_call(
            _ffn_kernel, grid=(rows // tm,), in_specs=in_specs, out_specs=row_spec,
            out_shape=jax.ShapeDtypeStruct((rows, D_MODEL), F32), compiler_params=params,
            name="ffn")(x, w13, w2, g, b)
    pw = wp.shape[1]
    return pl.pallas_call(
        functools.partial(_ffn_proj_kernel, proj_scale=proj_scale), grid=(rows // tm,),
        in_specs=in_specs + [_resident(wp.shape)],
        out_specs=[row_spec, pl.BlockSpec((tm, pw), lambda i: (i, 0))],
        out_shape=[jax.ShapeDtypeStruct((rows, D_MODEL), F32), jax.ShapeDtypeStruct((rows, pw), F32)],
        compiler_params=params, name="ffn_proj")(x, w13, w2, g, b, wp)


def _oproj_kernel(x_ref, a_ref, wo_ref, g_ref, b_ref, o_ref):
    mix = _dot(a_ref[...].astype(BF16), wo_ref[...])
    o_ref[...] = _layer_norm(ALPHA * x_ref[...] + mix, g_ref[...], b_ref[...])


def _oproj(x, a, wo, g, b):
    rows = x.shape[0]
    tm = _row_tile(rows)
    row_spec = pl.BlockSpec((tm, D_MODEL), lambda i: (i, 0))
    return pl.pallas_call(
        _oproj_kernel, grid=(rows // tm,),
        in_specs=[row_spec, pl.BlockSpec((tm, ATTN_GROUP_WIDTH), lambda i: (i, 0)),
                  _resident(wo.shape), _resident(g.shape), _resident(b.shape)],
        out_specs=row_spec, out_shape=jax.ShapeDtypeStruct((rows, D_MODEL), F32),
        compiler_params=pltpu.CompilerParams(dimension_semantics=("arbitrary",),
                                             vmem_limit_bytes=VMEM_LIMIT_BYTES),
        name="oproj")(x, a, wo, g, b)


CARRY_ROWS = 16


def _pool_tail(x, u, pooled_groups, wgrp_ref, scale_ref, wout_ref, g_ref, b_ref):
    zs = [_dot(p.astype(BF16), wgrp_ref[gi]) for gi, p in enumerate(pooled_groups)]
    z = jnp.concatenate(zs, axis=-1) * scale_ref[...]
    mix = _dot(z.astype(BF16), wout_ref[...])
    return _layer_norm(ALPHA * x + mix, g_ref[...], b_ref[...])


def _pool_prompt_kernel(x_ref, win_ref, wgrp_ref, scale_ref, wout_ref, g_ref, b_ref, o_ref, st_ref, ue_ref,
                        *, tm, tiles):
    t = pl.program_id(1)

    @pl.when(t == 0)
    def _():
        ue_ref[0:CARRY_ROWS, :] = jnp.zeros((CARRY_ROWS, D_MODEL), F32)

    x = x_ref[...]
    u = _dot(x.astype(BF16), win_ref[...])
    ue_ref[CARRY_ROWS:CARRY_ROWS + tm, :] = u
    pos = t * tm + lax.broadcasted_iota(jnp.int32, (tm, 1), 0)
    pooled = []
    for gi, w in enumerate(POOL_WINDOWS):
        cols = slice(gi * POOL_GROUP_DIM, (gi + 1) * POOL_GROUP_DIM)
        ug = u[:, cols]
        win_sum = ug
        for k in range(1, w):
            win_sum = win_sum + ue_ref[CARRY_ROWS - k:CARRY_ROWS - k + tm, cols]
        count = jnp.minimum(pos + 1, w).astype(F32)
        pooled.append(win_sum / count - ug)
    o_ref[...] = _pool_tail(x, u, pooled, wgrp_ref, scale_ref, wout_ref, g_ref, b_ref)

    tail = ue_ref[tm:tm + CARRY_ROWS, :]

    @pl.when(t == tiles - 1)
    def _():
        st_ref[0] = tail

    ue_ref[0:CARRY_ROWS, :] = tail


def _pool_prompt(x, batch, w_in, w_grp, scale, w_out, g, b):
    rows = x.shape[0]
    seq = rows // batch
    tm = _row_tile(seq)
    tiles = seq // tm
    row_spec = pl.BlockSpec((tm, D_MODEL), lambda bi, t: (bi * tiles + t, 0))
    return pl.pallas_call(
        functools.partial(_pool_prompt_kernel, tm=tm, tiles=tiles), grid=(batch, tiles),
        in_specs=[row_spec, _resident(w_in.shape), _resident(w_grp.shape), _resident(scale.shape),
                  _resident(w_out.shape), _resident(g.shape), _resident(b.shape)],
        out_specs=[row_spec, pl.BlockSpec((1, CARRY_ROWS, D_MODEL), lambda bi, t: (bi, 0, 0))],
        out_shape=[jax.ShapeDtypeStruct((rows, D_MODEL), F32),
                   jax.ShapeDtypeStruct((batch, CARRY_ROWS, D_MODEL), F32)],
        scratch_shapes=[pltpu.VMEM((tm + CARRY_ROWS, D_MODEL), F32)],
        compiler_params=pltpu.CompilerParams(dimension_semantics=("arbitrary", "arbitrary"),
                                             vmem_limit_bytes=VMEM_LIMIT_BYTES),
        name="pool_prompt")(x, w_in, w_grp, scale, w_out, g, b)


def _pool_sample_kernel(x_ref, st_ref, win_ref, wgrp_ref, scale_ref, wout_ref, g_ref, b_ref, o_ref, nst_ref,
                        *, steps, batch):
    x = x_ref[...]
    u = _dot(x.astype(BF16), win_ref[...])
    ext = [st_ref[k] for k in range(POOL_PAD)] + [u[i * batch:(i + 1) * batch, :] for i in range(steps)]
    pooled = []
    for gi, w in enumerate(POOL_WINDOWS):
        cols = slice(gi * POOL_GROUP_DIM, (gi + 1) * POOL_GROUP_DIM)
        per_step = []
        for i in range(steps):
            win_sum = ext[POOL_PAD + i][:, cols]
            for k in range(1, w):
                win_sum = win_sum + ext[POOL_PAD + i - k][:, cols]
            count = float(min(PAST_LEN + i + 1, w))
            per_step.append(win_sum / count - ext[POOL_PAD + i][:, cols])
        pooled.append(jnp.concatenate(per_step, axis=0))
    o_ref[...] = _pool_tail(x, u, pooled, wgrp_ref, scale_ref, wout_ref, g_ref, b_ref)
    for k in range(POOL_PAD):
        nst_ref[k] = ext[steps + k]


def _pool_sample(x, st, steps, batch, w_in, w_grp, scale, w_out, g, b):
    rows = x.shape[0]
    full = lambda shape: pl.BlockSpec(shape, lambda i: (0,) * len(shape))
    return pl.pallas_call(
        functools.partial(_pool_sample_kernel, steps=steps, batch=batch), grid=(1,),
        in_specs=[full(x.shape), full(st.shape), full(w_in.shape), full(w_grp.shape), full(scale.shape),
                  full(w_out.shape), full(g.shape), full(b.shape)],
        out_specs=[full(x.shape), full(st.shape)],
        out_shape=[jax.ShapeDtypeStruct((rows, D_MODEL), F32), jax.ShapeDtypeStruct(st.shape, F32)],
        compiler_params=pltpu.CompilerParams(dimension_semantics=("arbitrary",),
                                             vmem_limit_bytes=VMEM_LIMIT_BYTES),
        name="pool_sample")(x, st, w_in, w_grp, scale, w_out, g, b)


def _attn_prompt_kernel(q0_ref, q1_ref, q2_ref, k0_ref, k1_ref, k2_ref, v0_ref, v1_ref, v2_ref, bias_ref,
                        o_ref, kbuf, vbuf, oslab, lslab, btile):
    c = pl.program_id(2)

    @pl.when(c == 0)
    def _():
        for g in range(N_DIL_GROUPS):
            for hh in range(2):
                row = jnp.broadcast_to(bias_ref[g, 0, hh:hh + 1, :], (Q_BLOCK, 2 * Q_BLOCK))
                btile[g, hh] = pltpu.roll(row, 0, 1, stride=1, stride_axis=0)

    slot = c % 2
    cur_base = pl.multiple_of(slot * ATTN_CHUNK, ATTN_CHUNK)
    prev_base = pl.multiple_of((1 - slot) * ATTN_CHUNK, ATTN_CHUNK)
    q_refs = (q0_ref, q1_ref, q2_ref)
    k_refs = (k0_ref, k1_ref, k2_ref)
    v_refs = (v0_ref, v1_ref, v2_ref)

    @pl.when(c == 0)
    def _():
        zeros = jnp.zeros((ATTN_CHUNK, LANES), F32)
        for g in range(N_DIL_GROUPS):
            kbuf[g, pl.ds(prev_base, ATTN_CHUNK), :] = zeros
            vbuf[g, pl.ds(prev_base, ATTN_CHUNK), :] = zeros

    for g in range(N_DIL_GROUPS):
        kbuf[g, pl.ds(cur_base, ATTN_CHUNK), :] = k_refs[g][0]
        vbuf[g, pl.ds(cur_base, ATTN_CHUNK), :] = v_refs[g][0]

    lane = lax.broadcasted_iota(jnp.int32, (1, LANES), 1)
    first_head = lane < HEAD_DIM
    key_col = lax.broadcasted_iota(jnp.int32, (1, 2 * Q_BLOCK), 1)

    def strided(start, dil):
        return pl.ds(start, Q_BLOCK) if dil == 1 else pl.ds(start, Q_BLOCK, stride=dil)

    for g, (_, dil) in enumerate(DIL_PATTERNS):
        sub_blocks = ATTN_CHUNK // (Q_BLOCK * dil)
        span = Q_BLOCK * dil

        def block(blk, carry, g=g, dil=dil, sub_blocks=sub_blocks, span=span):
            res = blk // sub_blocks
            sb = blk % sub_blocks
            start = res + sb * span
            prev_start = jnp.where(sb > 0, cur_base + start - span,
                                   prev_base + res + (sub_blocks - 1) * span)
            qb = q_refs[g][0, strided(start, dil), :].astype(BF16)
            kk = jnp.concatenate([kbuf[g, strided(prev_start, dil), :],
                                  kbuf[g, strided(cur_base + start, dil), :]], axis=0).astype(BF16)
            vv = jnp.concatenate([vbuf[g, strided(prev_start, dil), :],
                                  vbuf[g, strided(cur_base + start, dil), :]], axis=0).astype(BF16)
            no_prev = jnp.logical_and(jnp.logical_and(c == 0, sb == 0), key_col < Q_BLOCK)
            outs, lses = [], []
            for hh in range(2):
                head_lanes = first_head if hh == 0 else jnp.logical_not(first_head)
                qh = jnp.where(head_lanes, qb, jnp.zeros_like(qb))
                s = lax.dot_general(qh, kk, (((1,), (1,)), ((), ())), preferred_element_type=F32)
                s = s + btile[g, hh]
                s = jnp.where(no_prev, NEG, s)
                m = jnp.max(s, axis=1, keepdims=True)
                p = jnp.exp(s - m)
                l = jnp.sum(p, axis=1, keepdims=True)
                pv = _dot(p.astype(BF16), vv)
                outs.append(pv / l)
                lses.append(m + jnp.log(l))
            oslab[g, strided(start, dil), :] = jnp.where(first_head, outs[0], outs[1])
            lslab[g, strided(start, dil), :] = jnp.where(first_head, lses[0], lses[1])
            return carry

        lax.fori_loop(0, ATTN_CHUNK // Q_BLOCK, block, 0, unroll=BLOCK_UNROLL)

    def combine(i, carry):
        rows = pl.ds(pl.multiple_of(i * MXU_DIM, MXU_DIM), MXU_DIM)
        ls = [lslab[g, rows, :] for g in range(N_DIL_GROUPS)]
        top = jnp.maximum(jnp.maximum(ls[0], ls[1]), ls[2])
        es = [jnp.exp(l - top) for l in ls]
        den = es[0] + es[1] + es[2]
        num = es[0] * oslab[0, rows, :] + es[1] * oslab[1, rows, :] + es[2] * oslab[2, rows, :]
        o_ref[0, rows, :] = num / den
        return carry

    lax.fori_loop(0, ATTN_CHUNK // MXU_DIM, combine, 0)


def _group_bias(rel_bias, g):
    buckets = _bucket_table(np.arange(N_KEYS) * DIL_PATTERNS[g][1])
    return rel_bias[buckets][:, g * HEADS_PER_GROUP:(g + 1) * HEADS_PER_GROUP]


def _prompt_bias_rows(rel_bias):
    rows = []
    for g in range(N_DIL_GROUPS):
        by_key = _group_bias(rel_bias, g)[::-1].T
        rows.append(jnp.pad(by_key, ((0, 0), (0, 2 * Q_BLOCK - N_KEYS)), constant_values=NEG))
    return jnp.stack(rows).reshape(N_DIL_GROUPS, HEADS_PER_GROUP // 2, 2, 2 * Q_BLOCK).astype(F32)


def _attn_prompt(q, kv, bias_rows):
    batch, seq, _ = q.shape
    chunks = seq // ATTN_CHUNK
    pairs = ATTN_GROUP_WIDTH // LANES
    blk = (1, ATTN_CHUNK, LANES)
    q_specs = [pl.BlockSpec(blk, functools.partial(lambda b, hp, c, g: (b, c, g * pairs + hp), g=g))
               for g in range(N_DIL_GROUPS)]
    k_specs = [pl.BlockSpec(blk, functools.partial(lambda b, hp, c, g: (b, c, 2 * g * pairs + hp), g=g))
               for g in range(N_DIL_GROUPS)]
    v_specs = [pl.BlockSpec(blk, functools.partial(lambda b, hp, c, g: (b, c, (2 * g + 1) * pairs + hp), g=g))
               for g in range(N_DIL_GROUPS)]
    bias_spec = pl.BlockSpec((N_DIL_GROUPS, 1, 2, 2 * Q_BLOCK), lambda b, hp, c: (0, hp, 0, 0))
    return pl.pallas_call(
        _attn_prompt_kernel, grid=(batch, pairs, chunks),
        in_specs=q_specs + k_specs + v_specs + [bias_spec],
        out_specs=pl.BlockSpec(blk, lambda b, hp, c: (b, c, hp)),
        out_shape=jax.ShapeDtypeStruct((batch, seq, ATTN_GROUP_WIDTH), F32),
        scratch_shapes=[pltpu.VMEM((N_DIL_GROUPS, 2 * ATTN_CHUNK, LANES), F32),
                        pltpu.VMEM((N_DIL_GROUPS, 2 * ATTN_CHUNK, LANES), F32),
                        pltpu.VMEM((N_DIL_GROUPS, ATTN_CHUNK, LANES), F32),
                        pltpu.VMEM((N_DIL_GROUPS, ATTN_CHUNK, LANES), F32),
                        pltpu.VMEM((N_DIL_GROUPS, 2, Q_BLOCK, 2 * Q_BLOCK), F32)],
        compiler_params=pltpu.CompilerParams(dimension_semantics=("arbitrary", "arbitrary", "arbitrary"),
                                             vmem_limit_bytes=VMEM_LIMIT_BYTES),
        name="attn_prompt")(q, q, q, kv, kv, kv, kv, kv, kv, bias_rows)


def _expand_heads(x):
    width = x.shape[-1]
    return jnp.broadcast_to(x[:, None, :], (HEADS_PER_GROUP, HEAD_DIM, width)).reshape(ATTN_GROUP_WIDTH, width)


def _head_sums(x):
    return x.reshape(HEADS_PER_GROUP, HEAD_DIM, x.shape[-1]).sum(axis=1)


def _softmax_parts(s):
    m = jnp.max(s, axis=1, keepdims=True)
    p = jnp.exp(s - m)
    return m, p, jnp.sum(p, axis=1, keepdims=True)


def _attn_sample_kernel(q_ref, kv_ref, c0_ref, c1_ref, c2_ref, b0_ref, b1_ref, b2_ref, o_ref, *, steps):
    lane = lax.broadcasted_iota(jnp.int32, (1, LANES), 1)
    sub = lax.broadcasted_iota(jnp.int32, (LANES, 1), 0)
    group_cols = KV_WIDTH // N_DIL_GROUPS

    def lanes_from_rows(mat, period):
        res = sub & (period - 1)
        rows = jnp.zeros((LANES, ATTN_GROUP_WIDTH), F32)
        for i in range(steps):
            rows = jnp.where(res == i, mat[i:i + 1, :], rows)
        return rows.T

    def new_rows(g, which, period):
        lo = g * group_cols + which * ATTN_GROUP_WIDTH
        return lanes_from_rows(kv_ref[0, :, lo:lo + ATTN_GROUP_WIDTH], period)

    outs = [[None] * N_DIL_GROUPS for _ in range(steps)]
    stats = [[None] * N_DIL_GROUPS for _ in range(steps)]

    q0 = q_ref[0, :, 0:ATTN_GROUP_WIDTH]
    kn, vn = new_rows(0, 0, LANES), new_rows(0, 1, LANES)
    for i in range(steps):
        qpat = jnp.broadcast_to(q0[i:i + 1, :], (LANES, ATTN_GROUP_WIDTH)).T
        s = jnp.concatenate([_head_sums(c0_ref[0, 0] * qpat), _head_sums(kn * qpat)], axis=1) + b0_ref[i]
        m, p, l = _softmax_parts(s)
        acc = c0_ref[0, 1] * _expand_heads(p[:, :LANES]) + vn * _expand_heads(p[:, LANES:])
        outs[i][0] = jnp.sum(acc, axis=1, keepdims=True)
        stats[i][0] = (m, l)

    for g, c_ref, b_ref in ((1, c1_ref, b1_ref), (2, c2_ref, b2_ref)):
        win, dil = DIL_PATTERNS[g]
        tiles = win // LANES
        qpat = lanes_from_rows(q_ref[0, :, g * ATTN_GROUP_WIDTH:(g + 1) * ATTN_GROUP_WIDTH], dil)
        kn, vn = new_rows(g, 0, dil), new_rows(g, 1, dil)
        parts = [_head_sums(c_ref[0, 0, :, t * LANES:(t + 1) * LANES] * qpat) for t in range(tiles)]
        s_all = jnp.concatenate(parts + [_head_sums(kn * qpat)], axis=1) + b_ref[...]
        res_all = lax.broadcasted_iota(jnp.int32, (1, win + LANES), 1) & (dil - 1)
        p_all = jnp.zeros_like(s_all)
        for i in range(steps):
            m, p, l = _softmax_parts(jnp.where(res_all == i, s_all, NEG))
            p_all = p_all + p
            stats[i][g] = (m, l)
        acc = vn * _expand_heads(p_all[:, win:])
        for t in range(tiles):
            acc = acc + c_ref[0, 1, :, t * LANES:(t + 1) * LANES] * _expand_heads(p_all[:, t * LANES:(t + 1) * LANES])
        res = lane & (dil - 1)
        for i in range(steps):
            outs[i][g] = jnp.sum(jnp.where(res == i, acc, 0.0), axis=1, keepdims=True)

    cols = jnp.zeros((ATTN_GROUP_WIDTH, LANES), F32)
    for i in range(steps):
        lses = [m + jnp.log(l) for m, l in stats[i]]
        top = jnp.maximum(jnp.maximum(lses[0], lses[1]), lses[2])
        es = [jnp.exp(x - top) for x in lses]
        den = es[0] + es[1] + es[2]
        col = sum(outs[i][g] * _expand_heads(es[g] / stats[i][g][1]) for g in range(N_DIL_GROUPS))
        cols = jnp.where(lane == i, col / _expand_heads(den), cols)
    o_ref[0] = cols.T[0:steps, :]


def _sample_bias(rel_bias, steps):
    by_dist = [_group_bias(rel_bias, g).T for g in range(N_DIL_GROUPS)]
    win0 = DIL_PATTERNS[0][0]
    far_to_near = by_dist[0][:, ::-1]
    per_query = []
    for i in range(steps):
        cache = jnp.pad(far_to_near[:, :win0 - i], ((0, 0), (i, LANES - win0)), constant_values=NEG)
        new = jnp.pad(by_dist[0][:, :i + 1][:, ::-1], ((0, 0), (0, LANES - 1 - i)), constant_values=NEG)
        per_query.append(jnp.concatenate([cache, new], axis=1))
    tables = [jnp.stack(per_query).astype(F32)]
    for g in (1, 2):
        win, dil = DIL_PATTERNS[g]
        cache = jnp.repeat(by_dist[g][:, :0:-1], dil, axis=1)
        cache = jnp.where((np.arange(win) % dil < steps)[None], cache, NEG)
        new = jnp.pad(jnp.broadcast_to(by_dist[g][:, 0:1], (HEADS_PER_GROUP, steps)),
                      ((0, 0), (0, LANES - steps)), constant_values=NEG)
        tables.append(jnp.concatenate([cache, new], axis=1).astype(F32))
    return tables


def _attn_sample(q, kvn, caches, bias_tables):
    batch, steps, _ = q.shape
    views, specs = [], []
    for g, (win, dil) in enumerate(DIL_PATTERNS):
        assert caches[g].shape[1] == win and win % LANES == 0 and (g == 0 or steps <= dil)
        views.append(jnp.transpose(caches[g], (0, 2, 3, 4, 1)).reshape(batch, 2, ATTN_GROUP_WIDTH, win))
        specs.append(pl.BlockSpec((1, 2, ATTN_GROUP_WIDTH, win), lambda b: (b, 0, 0, 0)))
    whole = lambda a: pl.BlockSpec(a.shape, lambda b: (0,) * a.ndim)
    return pl.pallas_call(
        functools.partial(_attn_sample_kernel, steps=steps), grid=(batch,),
        in_specs=[pl.BlockSpec((1,) + q.shape[1:], lambda b: (b, 0, 0)),
                  pl.BlockSpec((1,) + kvn.shape[1:], lambda b: (b, 0, 0))] + specs
                 + [whole(t) for t in bias_tables],
        out_specs=pl.BlockSpec((1, steps, ATTN_GROUP_WIDTH), lambda b: (b, 0, 0)),
        out_shape=jax.ShapeDtypeStruct((batch, steps, ATTN_GROUP_WIDTH), F32),
        compiler_params=pltpu.CompilerParams(dimension_semantics=("arbitrary",),
                                             vmem_limit_bytes=VMEM_LIMIT_BYTES),
        name="attn_sample")(q, kvn, *views, *bias_tables)


def kernel(x_prompt, x_sample, state_pool, cache_kv_w128, cache_kv_w512, cache_kv_w2048, ln_g, ln_b, ffn_w13,
           ffn_w2, pool_w_in, pool_w_grp, pool_scale, pool_w_out, attn_w_kv, attn_w_q, attn_w_o, rel_bias):
    batch, seq, _ = x_prompt.shape
    dec_batch, steps, _ = x_sample.shape
    assert DEPTH == 2 and state_pool.shape[0] == 1 and attn_w_q.shape[0] == 1
    assert seq % ATTN_CHUNK == 0 and seq >= DIL_PATTERNS[-1][0]

    w13 = ffn_w13.astype(BF16)
    w2 = ffn_w2.astype(BF16)
    w_in = pool_w_in[0].astype(BF16)
    w_grp = pool_w_grp[0].astype(BF16)
    w_out = pool_w_out[0].astype(BF16)
    w_kv = attn_w_kv.astype(BF16)
    w_q = attn_w_q[0].astype(BF16)
    w_o = attn_w_o[0].astype(BF16)
    scale = pool_scale[0][None]
    ln = lambda layer, k: (ln_g[layer, k][None], ln_b[layer, k][None])
    q_scale = HEAD_DIM ** -0.5

    def layer0_ffn_a(x):
        return _ffn(x, w13[0, 0], w2[0, 0], *ln(0, 0))

    def layer0_ffn_b(x):
        return _ffn(x, w13[0, 1], w2[0, 1], *ln(0, 2), wp=w_kv)

    def layer1_ffn_a(x):
        return _ffn(x, w13[1, 0], w2[1, 0], *ln(1, 0), wp=w_q, proj_scale=q_scale)

    def layer1_tail(x, attn):
        x = _oproj(x, attn, w_o, *ln(1, 1))
        return _ffn(x, w13[1, 1], w2[1, 1], *ln(1, 2))

    xp = x_prompt.reshape(batch * seq, D_MODEL)
    xp = layer0_ffn_a(xp)
    xp, pool_p = _pool_prompt(xp, batch, w_in, w_grp, scale, w_out, *ln(0, 1))
    xp, kv_p = layer0_ffn_b(xp)
    xp, q_p = layer1_ffn_a(xp)
    kv_p = kv_p.reshape(batch, seq, KV_WIDTH)
    attn_p = _attn_prompt(q_p.reshape(batch, seq, Q_WIDTH), kv_p, _prompt_bias_rows(rel_bias))
    y_prompt = layer1_tail(xp, attn_p.reshape(batch * seq, ATTN_GROUP_WIDTH)).reshape(batch, seq, D_MODEL)
    pool_p = pool_p[:, CARRY_ROWS - POOL_PAD:][None]
    group_cols = KV_WIDTH // N_DIL_GROUPS
    kv_p_groups = [kv_p[:, seq - w:, g * group_cols:(g + 1) * group_cols]
                   .reshape(batch, w, 2, HEADS_PER_GROUP, HEAD_DIM) for g, (w, _) in enumerate(DIL_PATTERNS)]

    xs = jnp.swapaxes(x_sample, 0, 1).reshape(steps * dec_batch, D_MODEL)
    xs = layer0_ffn_a(xs)
    xs, pool_s = _pool_sample(xs, jnp.swapaxes(state_pool[0], 0, 1), steps, dec_batch,
                              w_in, w_grp, scale, w_out, *ln(0, 1))
    xs, kv_s = layer0_ffn_b(xs)
    xs, q_s = layer1_ffn_a(xs)
    kv_s = jnp.swapaxes(kv_s.reshape(steps, dec_batch, KV_WIDTH), 0, 1)
    q_s = jnp.swapaxes(q_s.reshape(steps, dec_batch, Q_WIDTH), 0, 1)
    attn_s = _attn_sample(q_s, kv_s, (cache_kv_w128, cache_kv_w512, cache_kv_w2048),
                          _sample_bias(rel_bias, steps))
    attn_s = jnp.swapaxes(attn_s, 0, 1).reshape(steps * dec_batch, ATTN_GROUP_WIDTH)
    y_sample = jnp.swapaxes(layer1_tail(xs, attn_s).reshape(steps, dec_batch, D_MODEL), 0, 1)
    pool_s = jnp.swapaxes(pool_s, 0, 1)[None]
    kv_s = kv_s.reshape(dec_batch, steps, N_DIL_GROUPS, 2, HEADS_PER_GROUP, HEAD_DIM)

    return (y_prompt, y_sample, pool_p, pool_s, kv_p_groups[0], kv_p_groups[1], kv_p_groups[2],
            kv_s[:, :, 0], kv_s[:, :, 1], kv_s[:, :, 2])
```

```python
import functools

import numpy as np
import jax
import jax.numpy as jnp
from jax import lax
from jax.experimental import pallas as pl
from jax.experimental.pallas import tpu as pltpu

F32 = jnp.float32
BF16 = jnp.bfloat16

D_MODEL = 1024
DEPTH = 2
PAST_LEN = 16384
POOL_WINDOWS = (2, 4, 8, 16)
POOL_GROUP_DIM = D_MODEL // len(POOL_WINDOWS)
POOL_PAD = max(POOL_WINDOWS) - 1
DIL_PATTERNS = ((128, 1), (512, 4), (2048, 16))
N_DIL_GROUPS = len(DIL_PATTERNS)
HEAD_DIM = 64
HEADS_PER_GROUP = 8
ATTN_GROUP_WIDTH = HEADS_PER_GROUP * HEAD_DIM
Q_WIDTH = N_DIL_GROUPS * ATTN_GROUP_WIDTH
KV_WIDTH = 2 * Q_WIDTH
N_BUCKETS = 32
MAX_DISTANCE = 2048
D_FF = 2816
ALPHA = (2.0 * DEPTH) ** 0.25
LN_EPS = 1e-5
NEG = -1e30
N_KEYS = 129

LANES = 128
MXU_DIM = 256
VMEM_LIMIT_BYTES = 56 * 1024 * 1024

FF_CHUNK = MXU_DIM
ROW_TILE = 512
Q_BLOCK = 128
ATTN_CHUNK = 2048
BLOCK_UNROLL = 8


def _bucket_table(dist):
    dist = np.asarray(dist, dtype=np.int64)
    max_exact = N_BUCKETS // 2
    ratio = np.maximum(dist, 1).astype(np.float64) / max_exact
    large = max_exact + (np.log(ratio) / np.log(MAX_DISTANCE / max_exact) * (N_BUCKETS - max_exact)).astype(np.int64)
    large = np.minimum(large, N_BUCKETS - 1)
    return np.where(dist < max_exact, dist, large).astype(np.int32)


def _layer_norm(z, g, b):
    mu = jnp.mean(z, axis=-1, keepdims=True)
    zc = z - mu
    var = jnp.mean(zc * zc, axis=-1, keepdims=True)
    return zc * lax.rsqrt(var + LN_EPS) * g + b


def _dot(a, b):
    return jnp.dot(a, b, preferred_element_type=F32)


def _swiglu(xb, w13_ref, w2_ref):
    acc = None
    for j in range(D_FF // FF_CHUNK):
        lo = j * FF_CHUNK
        gate = _dot(xb, w13_ref[:, lo:lo + FF_CHUNK])
        up = _dot(xb, w13_ref[:, D_FF + lo:D_FF + lo + FF_CHUNK])
        act = (gate * jax.nn.sigmoid(gate) * up).astype(BF16)
        y = _dot(act, w2_ref[lo:lo + FF_CHUNK, :])
        acc = y if acc is None else acc + y
    return acc


def _ffn_kernel(x_ref, w13_ref, w2_ref, g_ref, b_ref, o_ref):
    x = x_ref[...]
    y = _swiglu(x.astype(BF16), w13_ref, w2_ref)
    o_ref[...] = _layer_norm(ALPHA * x + 0.5 * y, g_ref[...], b_ref[...])


def _ffn_proj_kernel(x_ref, w13_ref, w2_ref, g_ref, b_ref, wp_ref, o_ref, p_ref, *, proj_scale):
    x = x_ref[...]
    y = _swiglu(x.astype(BF16), w13_ref, w2_ref)
    out = _layer_norm(ALPHA * x + 0.5 * y, g_ref[...], b_ref[...])
    o_ref[...] = out
    p_ref[...] = _dot(out.astype(BF16), wp_ref[...]) * proj_scale


def _resident(shape):
    nd = len(shape)
    return pl.BlockSpec(shape, lambda *_: (0,) * nd, pipeline_mode=pl.Buffered(1))


def _row_tile(rows):
    return ROW_TILE if rows % ROW_TILE == 0 else rows


def _ffn(x, w13, w2, g, b, wp=None, proj_scale=1.0):
    rows = x.shape[0]
    tm = _row_tile(rows)
    row_spec = pl.BlockSpec((tm, D_MODEL), lambda i: (i, 0))
    in_specs = [row_spec, _resident(w13.shape), _resident(w2.shape), _resident(g.shape), _resident(b.shape)]
    params = pltpu.CompilerParams(dimension_semantics=("arbitrary",), vmem_limit_bytes=VMEM_LIMIT_BYTES)
    if wp is None:
        return pl.pallas_call(
            _ffn_kernel, grid=(rows // tm,), in_specs=in_specs, out_specs=row_spec,
            out_shape=jax.ShapeDtypeStruct((rows, D_MODEL), F32), compiler_params=params,
            name="ffn")(x, w13, w2, g, b)
    pw = wp.shape[1]
    return pl.pallas_call(
        functools.partial(_ffn_proj_kernel, proj_scale=proj_scale), grid=(rows // tm,),
        in_specs=in_specs + [_resident(wp.shape)],
        out_specs=[row_spec, pl.BlockSpec((tm, pw), lambda i: (i, 0))],
        out_shape=[jax.ShapeDtypeStruct((rows, D_MODEL), F32), jax.ShapeDtypeStruct((rows, pw), F32)],
        compiler_params=params, name="ffn_proj")(x, w13, w2, g, b, wp)


def _oproj_ffn_kernel(x_ref, a_ref, wo_ref, g1_ref, b1_ref, w13_ref, w2_ref, g2_ref, b2_ref, o_ref):
    mix = _dot(a_ref[...].astype(BF16), wo_ref[...])
    x = _layer_norm(ALPHA * x_ref[...] + mix, g1_ref[...], b1_ref[...])
    y = _swiglu(x.astype(BF16), w13_ref, w2_ref)
    o_ref[...] = _layer_norm(ALPHA * x + 0.5 * y, g2_ref[...], b2_ref[...])


def _oproj_ffn(x, a, wo, g1, b1, w13, w2, g2, b2):
    rows = x.shape[0]
    tm = _row_tile(rows)
    row_spec = pl.BlockSpec((tm, D_MODEL), lambda i: (i, 0))
    weights = (wo, g1, b1, w13, w2, g2, b2)
    return pl.pallas_call(
        _oproj_ffn_kernel, grid=(rows // tm,),
        in_specs=[row_spec, pl.BlockSpec((tm, ATTN_GROUP_WIDTH), lambda i: (i, 0))]
                 + [_resident(w.shape) for w in weights],
        out_specs=row_spec, out_shape=jax.ShapeDtypeStruct((rows, D_MODEL), F32),
        compiler_params=pltpu.CompilerParams(dimension_semantics=("arbitrary",),
                                             vmem_limit_bytes=VMEM_LIMIT_BYTES),
        name="oproj_ffn")(x, a, *weights)


CARRY_ROWS = 16


def _pool_tail(x, u, pooled_groups, wgrp_ref, scale_ref, wout_ref, g_ref, b_ref):
    zs = [_dot(p.astype(BF16), wgrp_ref[gi]) for gi, p in enumerate(pooled_groups)]
    z = jnp.concatenate(zs, axis=-1) * scale_ref[...]
    mix = _dot(z.astype(BF16), wout_ref[...])
    return _layer_norm(ALPHA * x + mix, g_ref[...], b_ref[...])


def _pool_prompt_kernel(x_ref, win_ref, wgrp_ref, scale_ref, wout_ref, g_ref, b_ref, o_ref, st_ref, ue_ref,
                        *, tm, tiles):
    t = pl.program_id(1)

    @pl.when(t == 0)
    def _():
        ue_ref[0:CARRY_ROWS, :] = jnp.zeros((CARRY_ROWS, D_MODEL), F32)

    x = x_ref[...]
    u = _dot(x.astype(BF16), win_ref[...])
    ue_ref[CARRY_ROWS:CARRY_ROWS + tm, :] = u
    pos = t * tm + lax.broadcasted_iota(jnp.int32, (tm, 1), 0)
    pooled = []
    for gi, w in enumerate(POOL_WINDOWS):
        cols = slice(gi * POOL_GROUP_DIM, (gi + 1) * POOL_GROUP_DIM)
        ug = u[:, cols]
        win_sum = ug
        for k in range(1, w):
            win_sum = win_sum + ue_ref[CARRY_ROWS - k:CARRY_ROWS - k + tm, cols]
        count = jnp.minimum(pos + 1, w).astype(F32)
        pooled.append(win_sum / count - ug)
    o_ref[...] = _pool_tail(x, u, pooled, wgrp_ref, scale_ref, wout_ref, g_ref, b_ref)

    tail = ue_ref[tm:tm + CARRY_ROWS, :]

    @pl.when(t == tiles - 1)
    def _():
        st_ref[0] = tail

    ue_ref[0:CARRY_ROWS, :] = tail


def _pool_prompt(x, batch, w_in, w_grp, scale, w_out, g, b):
    rows = x.shape[0]
    seq = rows // batch
    tm = _row_tile(seq)
    tiles = seq // tm
    row_spec = pl.BlockSpec((tm, D_MODEL), lambda bi, t: (bi * tiles + t, 0))
    return pl.pallas_call(
        functools.partial(_pool_prompt_kernel, tm=tm, tiles=tiles), grid=(batch, tiles),
        in_specs=[row_spec, _resident(w_in.shape), _resident(w_grp.shape), _resident(scale.shape),
                  _resident(w_out.shape), _resident(g.shape), _resident(b.shape)],
        out_specs=[row_spec, pl.BlockSpec((1, CARRY_ROWS, D_MODEL), lambda bi, t: (bi, 0, 0))],
        out_shape=[jax.ShapeDtypeStruct((rows, D_MODEL), F32),
                   jax.ShapeDtypeStruct((batch, CARRY_ROWS, D_MODEL), F32)],
        scratch_shapes=[pltpu.VMEM((tm + CARRY_ROWS, D_MODEL), F32)],
        compiler_params=pltpu.CompilerParams(dimension_semantics=("arbitrary", "arbitrary"),
                                             vmem_limit_bytes=VMEM_LIMIT_BYTES),
        name="pool_prompt")(x, w_in, w_grp, scale, w_out, g, b)


def _pool_sample_kernel(x_ref, st_ref, win_ref, wgrp_ref, scale_ref, wout_ref, g_ref, b_ref, o_ref, nst_ref,
                        *, steps, batch):
    x = x_ref[...]
    u = _dot(x.astype(BF16), win_ref[...])
    ext = [st_ref[k] for k in range(POOL_PAD)] + [u[i * batch:(i + 1) * batch, :] for i in range(steps)]
    pooled = []
    for gi, w in enumerate(POOL_WINDOWS):
        cols = slice(gi * POOL_GROUP_DIM, (gi + 1) * POOL_GROUP_DIM)
        per_step = []
        for i in range(steps):
            win_sum = ext[POOL_PAD + i][:, cols]
            for k in range(1, w):
                win_sum = win_sum + ext[POOL_PAD + i - k][:, cols]
            count = float(min(PAST_LEN + i + 1, w))
            per_step.append(win_sum / count - ext[POOL_PAD + i][:, cols])
        pooled.append(jnp.concatenate(per_step, axis=0))
    o_ref[...] = _pool_tail(x, u, pooled, wgrp_ref, scale_ref, wout_ref, g_ref, b_ref)
    for k in range(POOL_PAD):
        nst_ref[k] = ext[steps + k]


def _pool_sample(x, st, steps, batch, w_in, w_grp, scale, w_out, g, b):
    rows = x.shape[0]
    full = lambda shape: pl.BlockSpec(shape, lambda i: (0,) * len(shape))
    return pl.pallas_call(
        functools.partial(_pool_sample_kernel, steps=steps, batch=batch), grid=(1,),
        in_specs=[full(x.shape), full(st.shape), full(w_in.shape), full(w_grp.shape), full(scale.shape),
                  full(w_out.shape), full(g.shape), full(b.shape)],
        out_specs=[full(x.shape), full(st.shape)],
        out_shape=[jax.ShapeDtypeStruct((rows, D_MODEL), F32), jax.ShapeDtypeStruct(st.shape, F32)],
        compiler_params=pltpu.CompilerParams(dimension_semantics=("arbitrary",),
                                             vmem_limit_bytes=VMEM_LIMIT_BYTES),
        name="pool_sample")(x, st, w_in, w_grp, scale, w_out, g, b)


def _attn_prompt_kernel(q0_ref, q1_ref, q2_ref, k0_ref, k1_ref, k2_ref, v0_ref, v1_ref, v2_ref, bias_ref,
                        o_ref, kbuf, vbuf, oslab, lslab, btile):
    c = pl.program_id(2)
    key_col = lax.broadcasted_iota(jnp.int32, (1, 2 * Q_BLOCK), 1)

    @pl.when(c == 0)
    def _():
        for g in range(N_DIL_GROUPS):
            for hh in range(2):
                row = jnp.broadcast_to(bias_ref[g, 0, hh:hh + 1, :], (Q_BLOCK, 2 * Q_BLOCK))
                tile = pltpu.roll(row, 0, 1, stride=1, stride_axis=0)
                btile[g, hh, 0] = tile
                btile[g, hh, 1] = jnp.where(key_col < Q_BLOCK, NEG, tile)

    slot = c % 2
    cur_base = pl.multiple_of(slot * ATTN_CHUNK, ATTN_CHUNK)
    prev_base = pl.multiple_of((1 - slot) * ATTN_CHUNK, ATTN_CHUNK)
    q_refs = (q0_ref, q1_ref, q2_ref)
    k_refs = (k0_ref, k1_ref, k2_ref)
    v_refs = (v0_ref, v1_ref, v2_ref)

    @pl.when(c == 0)
    def _():
        zeros = jnp.zeros((ATTN_CHUNK, LANES), F32)
        for g in range(N_DIL_GROUPS):
            kbuf[g, pl.ds(prev_base, ATTN_CHUNK), :] = zeros
            vbuf[g, pl.ds(prev_base, ATTN_CHUNK), :] = zeros

    for g in range(N_DIL_GROUPS):
        kbuf[g, pl.ds(cur_base, ATTN_CHUNK), :] = k_refs[g][0]
        vbuf[g, pl.ds(cur_base, ATTN_CHUNK), :] = v_refs[g][0]

    lane = lax.broadcasted_iota(jnp.int32, (1, LANES), 1)
    first_head = lane < HEAD_DIM

    def strided(start, dil):
        return pl.ds(start, Q_BLOCK) if dil == 1 else pl.ds(start, Q_BLOCK, stride=dil)

    for g, (_, dil) in enumerate(DIL_PATTERNS):
        sub_blocks = ATTN_CHUNK // (Q_BLOCK * dil)
        span = Q_BLOCK * dil

        def block(blk, carry, g=g, dil=dil, sub_blocks=sub_blocks, span=span):
            res = blk // sub_blocks
            sb = blk % sub_blocks
            start = res + sb * span
            prev_start = jnp.where(sb > 0, cur_base + start - span,
                                   prev_base + res + (sub_blocks - 1) * span)
            qb = q_refs[g][0, strided(start, dil), :].astype(BF16)
            kk = jnp.concatenate([kbuf[g, strided(prev_start, dil), :],
                                  kbuf[g, strided(cur_base + start, dil), :]], axis=0).astype(BF16)
            vv = jnp.concatenate([vbuf[g, strided(prev_start, dil), :],
                                  vbuf[g, strided(cur_base + start, dil), :]], axis=0).astype(BF16)
            no_prev = jnp.logical_and(c == 0, sb == 0).astype(jnp.int32)
            outs, lses = [], []
            for hh in range(2):
                head_lanes = first_head if hh == 0 else jnp.logical_not(first_head)
                qh = jnp.where(head_lanes, qb, jnp.zeros_like(qb))
                s = lax.dot_general(qh, kk, (((1,), (1,)), ((), ())), preferred_element_type=F32)
                s = s + btile[g, hh, no_prev]
                m = jnp.max(s, axis=1, keepdims=True)
                p = jnp.exp(s - m)
                l = jnp.sum(p, axis=1, keepdims=True)
                pv = _dot(p.astype(BF16), vv)
                outs.append(pv * (1.0 / l))
                lses.append(m + jnp.log(l))
            oslab[g, strided(start, dil), :] = jnp.where(first_head, outs[0], outs[1])
            lslab[g, strided(start, dil), :] = jnp.where(first_head, lses[0], lses[1])
            return carry

        lax.fori_loop(0, ATTN_CHUNK // Q_BLOCK, block, 0, unroll=BLOCK_UNROLL)

    def combine(i, carry):
        rows = pl.ds(pl.multiple_of(i * MXU_DIM, MXU_DIM), MXU_DIM)
        ls = [lslab[g, rows, :] for g in range(N_DIL_GROUPS)]
        top = jnp.maximum(jnp.maximum(ls[0], ls[1]), ls[2])
        es = [jnp.exp(l - top) for l in ls]
        den = es[0] + es[1] + es[2]
        num = es[0] * oslab[0, rows, :] + es[1] * oslab[1, rows, :] + es[2] * oslab[2, rows, :]
        o_ref[0, rows, :] = num / den
        return carry

    lax.fori_loop(0, ATTN_CHUNK // MXU_DIM, combine, 0)


def _group_bias(rel_bias, g):
    buckets = _bucket_table(np.arange(N_KEYS) * DIL_PATTERNS[g][1])
    return rel_bias[buckets][:, g * HEADS_PER_GROUP:(g + 1) * HEADS_PER_GROUP]


def _prompt_bias_rows(rel_bias):
    rows = []
    for g in range(N_DIL_GROUPS):
        by_key = _group_bias(rel_bias, g)[::-1].T
        rows.append(jnp.pad(by_key, ((0, 0), (0, 2 * Q_BLOCK - N_KEYS)), constant_values=NEG))
    return jnp.stack(rows).reshape(N_DIL_GROUPS, HEADS_PER_GROUP // 2, 2, 2 * Q_BLOCK).astype(F32)


def _attn_prompt(q, kv, bias_rows):
    batch, seq, _ = q.shape
    chunks = seq // ATTN_CHUNK
    pairs = ATTN_GROUP_WIDTH // LANES
    blk = (1, ATTN_CHUNK, LANES)
    q_specs = [pl.BlockSpec(blk, functools.partial(lambda b, hp, c, g: (b, c, g * pairs + hp), g=g))
               for g in range(N_DIL_GROUPS)]
    k_specs = [pl.BlockSpec(blk, functools.partial(lambda b, hp, c, g: (b, c, 2 * g * pairs + hp), g=g))
               for g in range(N_DIL_GROUPS)]
    v_specs = [pl.BlockSpec(blk, functools.partial(lambda b, hp, c, g: (b, c, (2 * g + 1) * pairs + hp), g=g))
               for g in range(N_DIL_GROUPS)]
    bias_spec = pl.BlockSpec((N_DIL_GROUPS, 1, 2, 2 * Q_BLOCK), lambda b, hp, c: (0, hp, 0, 0))
    return pl.pallas_call(
        _attn_prompt_kernel, grid=(batch, pairs, chunks),
        in_specs=q_specs + k_specs + v_specs + [bias_spec],
        out_specs=pl.BlockSpec(blk, lambda b, hp, c: (b, c, hp)),
        out_shape=jax.ShapeDtypeStruct((batch, seq, ATTN_GROUP_WIDTH), F32),
        scratch_shapes=[pltpu.VMEM((N_DIL_GROUPS, 2 * ATTN_CHUNK, LANES), F32),
                        pltpu.VMEM((N_DIL_GROUPS, 2 * ATTN_CHUNK, LANES), F32),
                        pltpu.VMEM((N_DIL_GROUPS, ATTN_CHUNK, LANES), F32),
                        pltpu.VMEM((N_DIL_GROUPS, ATTN_CHUNK, LANES), F32),
                        pltpu.VMEM((N_DIL_GROUPS, 2, 2, Q_BLOCK, 2 * Q_BLOCK), F32)],
        compiler_params=pltpu.CompilerParams(dimension_semantics=("arbitrary", "arbitrary", "arbitrary"),
                                             vmem_limit_bytes=VMEM_LIMIT_BYTES),
        name="attn_prompt")(q, q, q, kv, kv, kv, kv, kv, kv, bias_rows)


def _expand_heads(x):
    width = x.shape[-1]
    return jnp.broadcast_to(x[:, None, :], (HEADS_PER_GROUP, HEAD_DIM, width)).reshape(ATTN_GROUP_WIDTH, width)


def _head_sums(x):
    return x.reshape(HEADS_PER_GROUP, HEAD_DIM, x.shape[-1]).sum(axis=1)


def _softmax_parts(s):
    m = jnp.max(s, axis=1, keepdims=True)
    p = jnp.exp(s - m)
    return m, p, jnp.sum(p, axis=1, keepdims=True)


def _attn_sample_kernel(q_ref, kv_ref, c0_ref, c1_ref, c2_ref, b0_ref, b1_ref, b2_ref, o_ref, *, steps):
    lane = lax.broadcasted_iota(jnp.int32, (1, LANES), 1)
    sub = lax.broadcasted_iota(jnp.int32, (LANES, 1), 0)
    group_cols = KV_WIDTH // N_DIL_GROUPS

    def lanes_from_rows(mat, period):
        res = sub & (period - 1)
        rows = jnp.zeros((LANES, ATTN_GROUP_WIDTH), F32)
        for i in range(steps):
            rows = jnp.where(res == i, mat[i:i + 1, :], rows)
        return rows.T

    def new_rows(g, which, period):
        lo = g * group_cols + which * ATTN_GROUP_WIDTH
        return lanes_from_rows(kv_ref[0, :, lo:lo + ATTN_GROUP_WIDTH], period)

    outs = [[None] * N_DIL_GROUPS for _ in range(steps)]
    stats = [[None] * N_DIL_GROUPS for _ in range(steps)]

    q0 = q_ref[0, :, 0:ATTN_GROUP_WIDTH]
    kn, vn = new_rows(0, 0, LANES), new_rows(0, 1, LANES)
    for i in range(steps):
        qpat = jnp.broadcast_to(q0[i:i + 1, :], (LANES, ATTN_GROUP_WIDTH)).T
        s = jnp.concatenate([_head_sums(c0_ref[0, 0] * qpat), _head_sums(kn * qpat)], axis=1) + b0_ref[i]
        m, p, l = _softmax_parts(s)
        acc = c0_ref[0, 1] * _expand_heads(p[:, :LANES]) + vn * _expand_heads(p[:, LANES:])
        outs[i][0] = jnp.sum(acc, axis=1, keepdims=True)
        stats[i][0] = (m, l)

    for g, c_ref, b_ref in ((1, c1_ref, b1_ref), (2, c2_ref, b2_ref)):
        win, dil = DIL_PATTERNS[g]
        tiles = win // LANES
        qpat = lanes_from_rows(q_ref[0, :, g * ATTN_GROUP_WIDTH:(g + 1) * ATTN_GROUP_WIDTH], dil)
        kn, vn = new_rows(g, 0, dil), new_rows(g, 1, dil)
        parts = [_head_sums(c_ref[0, 0, :, t * LANES:(t + 1) * LANES] * qpat) for t in range(tiles)]
        s_all = jnp.concatenate(parts + [_head_sums(kn * qpat)], axis=1) + b_ref[...]
        res_all = lax.broadcasted_iota(jnp.int32, (1, win + LANES), 1) & (dil - 1)
        p_all = jnp.zeros_like(s_all)
        for i in range(steps):
            m, p, l = _softmax_parts(jnp.where(res_all == i, s_all, NEG))
            p_all = p_all + p
            stats[i][g] = (m, l)
        acc = vn * _expand_heads(p_all[:, win:])
        for t in range(tiles):
            acc = acc + c_ref[0, 1, :, t * LANES:(t + 1) * LANES] * _expand_heads(p_all[:, t * LANES:(t + 1) * LANES])
        res = lane & (dil - 1)
        for i in range(steps):
            outs[i][g] = jnp.sum(jnp.where(res == i, acc, 0.0), axis=1, keepdims=True)

    cols = jnp.zeros((ATTN_GROUP_WIDTH, LANES), F32)
    for i in range(steps):
        lses = [m + jnp.log(l) for m, l in stats[i]]
        top = jnp.maximum(jnp.maximum(lses[0], lses[1]), lses[2])
        es = [jnp.exp(x - top) for x in lses]
        den = es[0] + es[1] + es[2]
        col = sum(outs[i][g] * _expand_heads(es[g] / stats[i][g][1]) for g in range(N_DIL_GROUPS))
        cols = jnp.where(lane == i, col / _expand_heads(den), cols)
    o_ref[0] = cols.T[0:steps, :]


def _sample_bias(rel_bias, steps):
    by_dist = [_group_bias(rel_bias, g).T for g in range(N_DIL_GROUPS)]
    win0 = DIL_PATTERNS[0][0]
    far_to_near = by_dist[0][:, ::-1]
    per_query = []
    for i in range(steps):
        cache = jnp.pad(far_to_near[:, :win0 - i], ((0, 0), (i, LANES - win0)), constant_values=NEG)
        new = jnp.pad(by_dist[0][:, :i + 1][:, ::-1], ((0, 0), (0, LANES - 1 - i)), constant_values=NEG)
        per_query.append(jnp.concatenate([cache, new], axis=1))
    tables = [jnp.stack(per_query).astype(F32)]
    for g in (1, 2):
        win, dil = DIL_PATTERNS[g]
        cache = jnp.repeat(by_dist[g][:, :0:-1], dil, axis=1)
        cache = jnp.where((np.arange(win) % dil < steps)[None], cache, NEG)
        new = jnp.pad(jnp.broadcast_to(by_dist[g][:, 0:1], (HEADS_PER_GROUP, steps)),
                      ((0, 0), (0, LANES - steps)), constant_values=NEG)
        tables.append(jnp.concatenate([cache, new], axis=1).astype(F32))
    return tables


def _attn_sample(q, kvn, caches, bias_tables):
    batch, steps, _ = q.shape
    views, specs = [], []
    for g, (win, dil) in enumerate(DIL_PATTERNS):
        assert caches[g].shape[1] == win and win % LANES == 0 and (g == 0 or steps <= dil)
        views.append(jnp.transpose(caches[g], (0, 2, 3, 4, 1)).reshape(batch, 2, ATTN_GROUP_WIDTH, win))
        specs.append(pl.BlockSpec((1, 2, ATTN_GROUP_WIDTH, win), lambda b: (b, 0, 0, 0)))
    whole = lambda a: pl.BlockSpec(a.shape, lambda b: (0,) * a.ndim)
    return pl.pallas_call(
        functools.partial(_attn_sample_kernel, steps=steps), grid=(batch,),
        in_specs=[pl.BlockSpec((1,) + q.shape[1:], lambda b: (b, 0, 0)),
                  pl.BlockSpec((1,) + kvn.shape[1:], lambda b: (b, 0, 0))] + specs
                 + [whole(t) for t in bias_tables],
        out_specs=pl.BlockSpec((1, steps, ATTN_GROUP_WIDTH), lambda b: (b, 0, 0)),
        out_shape=jax.ShapeDtypeStruct((batch, steps, ATTN_GROUP_WIDTH), F32),
        compiler_params=pltpu.CompilerParams(dimension_semantics=("arbitrary",),
                                             vmem_limit_bytes=VMEM_LIMIT_BYTES),
        name="attn_sample")(q, kvn, *views, *bias_tables)


def kernel(x_prompt, x_sample, state_pool, cache_kv_w128, cache_kv_w512, cache_kv_w2048, ln_g, ln_b, ffn_w13,
           ffn_w2, pool_w_in, pool_w_grp, pool_scale, pool_w_out, attn_w_kv, attn_w_q, attn_w_o, rel_bias):
    batch, seq, _ = x_prompt.shape
    dec_batch, steps, _ = x_sample.shape
    assert DEPTH == 2 and state_pool.shape[0] == 1 and attn_w_q.shape[0] == 1
    assert seq % ATTN_CHUNK == 0 and seq >= DIL_PATTERNS[-1][0]

    w13 = [[ffn_w13[layer, k].astype(BF16) for k in range(2)] for layer in range(DEPTH)]
    w2 = [[ffn_w2[layer, k].astype(BF16) for k in range(2)] for layer in range(DEPTH)]
    w_in = pool_w_in[0].astype(BF16)
    w_grp = pool_w_grp[0].astype(BF16)
    w_out = pool_w_out[0].astype(BF16)
    w_kv = attn_w_kv.astype(BF16)
    w_q = attn_w_q[0].astype(BF16)
    w_o = attn_w_o[0].astype(BF16)
    scale = pool_scale[0][None]
    ln = lambda layer, k: (ln_g[layer, k][None], ln_b[layer, k][None])
    q_scale = HEAD_DIM ** -0.5

    def layer0_ffn_a(x):
        return _ffn(x, w13[0][0], w2[0][0], *ln(0, 0))

    def layer0_ffn_b(x):
        return _ffn(x, w13[0][1], w2[0][1], *ln(0, 2), wp=w_kv)

    def layer1_ffn_a(x):
        return _ffn(x, w13[1][0], w2[1][0], *ln(1, 0), wp=w_q, proj_scale=q_scale)

    def layer1_tail(x, attn):
        return _oproj_ffn(x, attn, w_o, *ln(1, 1), w13[1][1], w2[1][1], *ln(1, 2))

    xp = x_prompt.reshape(batch * seq, D_MODEL)
    xp = layer0_ffn_a(xp)
    xp, pool_p = _pool_prompt(xp, batch, w_in, w_grp, scale, w_out, *ln(0, 1))
    xp, kv_p = layer0_ffn_b(xp)
    xp, q_p = layer1_ffn_a(xp)
    kv_p = kv_p.reshape(batch, seq, KV_WIDTH)
    attn_p = _attn_prompt(q_p.reshape(batch, seq, Q_WIDTH), kv_p, _prompt_bias_rows(rel_bias))
    y_prompt = layer1_tail(xp, attn_p.reshape(batch * seq, ATTN_GROUP_WIDTH)).reshape(batch, seq, D_MODEL)
    pool_p = pool_p[:, CARRY_ROWS - POOL_PAD:][None]
    group_cols = KV_WIDTH // N_DIL_GROUPS
    kv_p_groups = [kv_p[:, seq - w:, g * group_cols:(g + 1) * group_cols]
                   .reshape(batch, w, 2, HEADS_PER_GROUP, HEAD_DIM) for g, (w, _) in enumerate(DIL_PATTERNS)]

    xs = jnp.swapaxes(x_sample, 0, 1).reshape(steps * dec_batch, D_MODEL)
    xs = layer0_ffn_a(xs)
    xs, pool_s = _pool_sample(xs, jnp.swapaxes(state_pool[0], 0, 1), steps, dec_batch,
                              w_in, w_grp, scale, w_out, *ln(0, 1))
    xs, kv_s = layer0_ffn_b(xs)
    xs, q_s = layer1_ffn_a(xs)
    kv_s = jnp.swapaxes(kv_s.reshape(steps, dec_batch, KV_WIDTH), 0, 1)
    q_s = jnp.swapaxes(q_s.reshape(steps, dec_batch, Q_WIDTH), 0, 1)
    attn_s = _attn_sample(q_s, kv_s, (cache_kv_w128, cache_kv_w512, cache_kv_w2048),
                          _sample_bias(rel_bias, steps))
    attn_s = jnp.swapaxes(attn_s, 0, 1).reshape(steps * dec_batch, ATTN_GROUP_WIDTH)
    y_sample = jnp.swapaxes(layer1_tail(xs, attn_s).reshape(steps, dec_batch, D_MODEL), 0, 1)
    pool_s = jnp.swapaxes(pool_s, 0, 1)[None]
    kv_s = kv_s.reshape(dec_batch, steps, N_DIL_GROUPS, 2, HEADS_PER_GROUP, HEAD_DIM)

    return (y_prompt, y_sample, pool_p, pool_s, kv_p_groups[0], kv_p_groups[1], kv_p_groups[2],
            kv_s[:, :, 0], kv_s[:, :, 1], kv_s[:, :, 2])
```

```python
import functools

import numpy as np
import jax
import jax.numpy as jnp
from jax import lax
from jax.experimental import pallas as pl
from jax.experimental.pallas import tpu as pltpu

F32 = jnp.float32
BF16 = jnp.bfloat16

D_MODEL = 1024
DEPTH = 2
PAST_LEN = 16384
POOL_WINDOWS = (2, 4, 8, 16)
POOL_GROUP_DIM = D_MODEL // len(POOL_WINDOWS)
POOL_PAD = max(POOL_WINDOWS) - 1
DIL_PATTERNS = ((128, 1), (512, 4), (2048, 16))
N_DIL_GROUPS = len(DIL_PATTERNS)
HEAD_DIM = 64
HEADS_PER_GROUP = 8
ATTN_GROUP_WIDTH = HEADS_PER_GROUP * HEAD_DIM
Q_WIDTH = N_DIL_GROUPS * ATTN_GROUP_WIDTH
KV_WIDTH = 2 * Q_WIDTH
N_BUCKETS = 32
MAX_DISTANCE = 2048
D_FF = 2816
ALPHA = (2.0 * DEPTH) ** 0.25
LN_EPS = 1e-5
NEG = -1e30
N_KEYS = 129

LANES = 128
MXU_DIM = 256
VMEM_LIMIT_BYTES = 56 * 1024 * 1024

FF_CHUNK = MXU_DIM
ROW_TILE = 512
Q_BLOCK = 128
ATTN_CHUNK = 2048
BLOCK_UNROLL = 8


def _bucket_table(dist):
    dist = np.asarray(dist, dtype=np.int64)
    max_exact = N_BUCKETS // 2
    ratio = np.maximum(dist, 1).astype(np.float64) / max_exact
    large = max_exact + (np.log(ratio) / np.log(MAX_DISTANCE / max_exact) * (N_BUCKETS - max_exact)).astype(np.int64)
    large = np.minimum(large, N_BUCKETS - 1)
    return np.where(dist < max_exact, dist, large).astype(np.int32)


def _layer_norm(z, g, b):
    mu = jnp.mean(z, axis=-1, keepdims=True)
    zc = z - mu
    var = jnp.mean(zc * zc, axis=-1, keepdims=True)
    return zc * lax.rsqrt(var + LN_EPS) * g + b


def _dot(a, b):
    return jnp.dot(a, b, preferred_element_type=F32)


def _swiglu(xb, w13_ref, w2_ref):
    acc = None
    for j in range(D_FF // FF_CHUNK):
        lo = j * FF_CHUNK
        gate = _dot(xb, w13_ref[:, lo:lo + FF_CHUNK])
        up = _dot(xb, w13_ref[:, D_FF + lo:D_FF + lo + FF_CHUNK])
        act = (gate * jax.nn.sigmoid(gate) * up).astype(BF16)
        y = _dot(act, w2_ref[lo:lo + FF_CHUNK, :])
        acc = y if acc is None else acc + y
    return acc


def _ffn_kernel(*refs, has_proj, proj_scale, n_cast):
    n_in = 5 + has_proj + n_cast
    x_ref, w13_ref, w2_ref, g_ref, b_ref = refs[:5]
    o_ref = refs[n_in]
    x = x_ref[...]
    y = _swiglu(x.astype(BF16), w13_ref, w2_ref)
    out = _layer_norm(ALPHA * x + 0.5 * y, g_ref[...], b_ref[...])
    o_ref[...] = out
    if has_proj:
        refs[n_in + 1][...] = _dot(out.astype(BF16), refs[5][...]) * proj_scale
    for src, dst in zip(refs[5 + has_proj:n_in], refs[n_in + 1 + has_proj:]):
        dst[...] = src[...].astype(BF16)


def _resident(shape):
    nd = len(shape)
    return pl.BlockSpec(shape, lambda *_: (0,) * nd, pipeline_mode=pl.Buffered(1))


def _row_tile(rows):
    return ROW_TILE if rows % ROW_TILE == 0 else rows


BF16_SUBLANES = 16


def _cast_chunks(n_rows, steps):
    chunks = steps
    while n_rows % chunks or (n_rows // chunks) % BF16_SUBLANES:
        chunks //= 2
    return chunks


def _ffn(x, w13, w2, g, b, wp=None, proj_scale=1.0, cast=()):
    rows = x.shape[0]
    tm = _row_tile(rows)
    steps = rows // tm
    row_spec = pl.BlockSpec((tm, D_MODEL), lambda i: (i, 0))
    operands = [x, w13, w2, g, b] + ([] if wp is None else [wp])
    in_specs = [row_spec] + [_resident(a.shape) for a in operands[1:]]
    out_specs = [row_spec]
    out_shape = [jax.ShapeDtypeStruct((rows, D_MODEL), F32)]
    if wp is not None:
        out_specs.append(pl.BlockSpec((tm, wp.shape[1]), lambda i: (i, 0)))
        out_shape.append(jax.ShapeDtypeStruct((rows, wp.shape[1]), F32))
    for w, lead in cast:
        n_rows, n_cols = w.shape[len(lead):]
        chunks = _cast_chunks(n_rows, steps)
        chunk_index = functools.partial(lambda i, last: (jnp.minimum(i, last), 0), last=chunks - 1)
        in_specs.append(pl.BlockSpec((None,) * len(lead) + (n_rows // chunks, n_cols),
                                     functools.partial(lambda i, lead, f: lead + f(i), lead=lead, f=chunk_index)))
        out_specs.append(pl.BlockSpec((n_rows // chunks, n_cols), chunk_index))
        out_shape.append(jax.ShapeDtypeStruct((n_rows, n_cols), BF16))
    outs = pl.pallas_call(
        functools.partial(_ffn_kernel, has_proj=wp is not None, proj_scale=proj_scale, n_cast=len(cast)),
        grid=(steps,), in_specs=in_specs, out_specs=out_specs, out_shape=out_shape,
        compiler_params=pltpu.CompilerParams(dimension_semantics=("arbitrary",),
                                             vmem_limit_bytes=VMEM_LIMIT_BYTES),
        name="ffn")(*operands, *[w for w, _ in cast])
    return outs[0] if len(outs) == 1 else outs


def _oproj_ffn_kernel(x_ref, a_ref, wo_ref, g1_ref, b1_ref, w13_ref, w2_ref, g2_ref, b2_ref, o_ref):
    mix = _dot(a_ref[...].astype(BF16), wo_ref[...])
    x = _layer_norm(ALPHA * x_ref[...] + mix, g1_ref[...], b1_ref[...])
    y = _swiglu(x.astype(BF16), w13_ref, w2_ref)
    o_ref[...] = _layer_norm(ALPHA * x + 0.5 * y, g2_ref[...], b2_ref[...])


def _oproj_ffn(x, a, wo, g1, b1, w13, w2, g2, b2):
    rows = x.shape[0]
    tm = _row_tile(rows)
    row_spec = pl.BlockSpec((tm, D_MODEL), lambda i: (i, 0))
    weights = (wo, g1, b1, w13, w2, g2, b2)
    return pl.pallas_call(
        _oproj_ffn_kernel, grid=(rows // tm,),
        in_specs=[row_spec, pl.BlockSpec((tm, ATTN_GROUP_WIDTH), lambda i: (i, 0))]
                 + [_resident(w.shape) for w in weights],
        out_specs=row_spec, out_shape=jax.ShapeDtypeStruct((rows, D_MODEL), F32),
        compiler_params=pltpu.CompilerParams(dimension_semantics=("arbitrary",),
                                             vmem_limit_bytes=VMEM_LIMIT_BYTES),
        name="oproj_ffn")(x, a, *weights)


CARRY_ROWS = 16
PAD_ROWS = 8


def _pool_tail(x, u, pooled_groups, wgrp_ref, scale_ref, wout_ref, g_ref, b_ref):
    zs = [_dot(p.astype(BF16), wgrp_ref[gi]) for gi, p in enumerate(pooled_groups)]
    z = jnp.concatenate(zs, axis=-1) * scale_ref[...]
    mix = _dot(z.astype(BF16), wout_ref[...])
    return _layer_norm(ALPHA * x + mix, g_ref[...], b_ref[...])


def _pool_prompt_kernel(x_ref, win_ref, wgrp_ref, scale_ref, wout_ref, g_ref, b_ref, o_ref, st_ref,
                        ue_ref, s2_ref, s4_ref, s8_ref, *, tm, tiles):
    t = pl.program_id(1)
    lo, base = PAD_ROWS, PAD_ROWS + CARRY_ROWS
    stages = (ue_ref, s2_ref, s4_ref, s8_ref)

    @pl.when(t == 0)
    def _():
        ue_ref[0:base, :] = jnp.zeros((base, D_MODEL), F32)
        for ref in stages[1:]:
            ref[0:lo, :] = jnp.zeros((lo, D_MODEL), F32)

    x = x_ref[...]
    u = _dot(x.astype(BF16), win_ref[...])
    ue_ref[base:base + tm, :] = u
    pos = t * tm + lax.broadcasted_iota(jnp.int32, (tm, 1), 0)
    pooled = []
    for gi, w in enumerate(POOL_WINDOWS):
        cols = slice(gi * POOL_GROUP_DIM, (gi + 1) * POOL_GROUP_DIM)
        n_stages = w.bit_length() - 1
        for level in range(n_stages - 1):
            src, dst, k = stages[level], stages[level + 1], 2 ** level
            dst[lo:base + tm, cols] = src[lo:base + tm, cols] + src[lo - k:base + tm - k, cols]
        src, k = stages[n_stages - 1], w // 2
        win_sum = src[base:base + tm, cols] + src[base - k:base + tm - k, cols]
        inv_count = 1.0 / jnp.minimum(pos + 1, w).astype(F32)
        pooled.append(win_sum * inv_count - u[:, cols])
    o_ref[...] = _pool_tail(x, u, pooled, wgrp_ref, scale_ref, wout_ref, g_ref, b_ref)

    tail = ue_ref[base + tm - CARRY_ROWS:base + tm, :]

    @pl.when(t == tiles - 1)
    def _():
        st_ref[0] = tail

    ue_ref[lo:base, :] = tail


def _pool_prompt(x, batch, w_in, w_grp, scale, w_out, g, b):
    rows = x.shape[0]
    seq = rows // batch
    tm = _row_tile(seq)
    tiles = seq // tm
    row_spec = pl.BlockSpec((tm, D_MODEL), lambda bi, t: (bi * tiles + t, 0))
    return pl.pallas_call(
        functools.partial(_pool_prompt_kernel, tm=tm, tiles=tiles), grid=(batch, tiles),
        in_specs=[row_spec, _resident(w_in.shape), _resident(w_grp.shape), _resident(scale.shape),
                  _resident(w_out.shape), _resident(g.shape), _resident(b.shape)],
        out_specs=[row_spec, pl.BlockSpec((1, CARRY_ROWS, D_MODEL), lambda bi, t: (bi, 0, 0))],
        out_shape=[jax.ShapeDtypeStruct((rows, D_MODEL), F32),
                   jax.ShapeDtypeStruct((batch, CARRY_ROWS, D_MODEL), F32)],
        scratch_shapes=[pltpu.VMEM((PAD_ROWS + CARRY_ROWS + tm, D_MODEL), F32)] * 4,
        compiler_params=pltpu.CompilerParams(dimension_semantics=("arbitrary", "arbitrary"),
                                             vmem_limit_bytes=VMEM_LIMIT_BYTES),
        name="pool_prompt")(x, w_in, w_grp, scale, w_out, g, b)


def _pool_sample_kernel(x_ref, st_ref, win_ref, wgrp_ref, scale_ref, wout_ref, g_ref, b_ref, o_ref, nst_ref,
                        *, steps, batch):
    x = x_ref[...]
    u = _dot(x.astype(BF16), win_ref[...])
    ext = [st_ref[k] for k in range(POOL_PAD)] + [u[i * batch:(i + 1) * batch, :] for i in range(steps)]
    pooled = []
    for gi, w in enumerate(POOL_WINDOWS):
        cols = slice(gi * POOL_GROUP_DIM, (gi + 1) * POOL_GROUP_DIM)
        per_step = []
        for i in range(steps):
            win_sum = ext[POOL_PAD + i][:, cols]
            for k in range(1, w):
                win_sum = win_sum + ext[POOL_PAD + i - k][:, cols]
            count = float(min(PAST_LEN + i + 1, w))
            per_step.append(win_sum / count - ext[POOL_PAD + i][:, cols])
        pooled.append(jnp.concatenate(per_step, axis=0))
    o_ref[...] = _pool_tail(x, u, pooled, wgrp_ref, scale_ref, wout_ref, g_ref, b_ref)
    for k in range(POOL_PAD):
        nst_ref[k] = ext[steps + k]


def _pool_sample(x, st, steps, batch, w_in, w_grp, scale, w_out, g, b):
    rows = x.shape[0]
    full = lambda shape: pl.BlockSpec(shape, lambda i: (0,) * len(shape))
    return pl.pallas_call(
        functools.partial(_pool_sample_kernel, steps=steps, batch=batch), grid=(1,),
        in_specs=[full(x.shape), full(st.shape), full(w_in.shape), full(w_grp.shape), full(scale.shape),
                  full(w_out.shape), full(g.shape), full(b.shape)],
        out_specs=[full(x.shape), full(st.shape)],
        out_shape=[jax.ShapeDtypeStruct((rows, D_MODEL), F32), jax.ShapeDtypeStruct(st.shape, F32)],
        compiler_params=pltpu.CompilerParams(dimension_semantics=("arbitrary",),
                                             vmem_limit_bytes=VMEM_LIMIT_BYTES),
        name="pool_sample")(x, st, w_in, w_grp, scale, w_out, g, b)


def _attn_prompt_kernel(q0_ref, q1_ref, q2_ref, k0_ref, k1_ref, k2_ref, v0_ref, v1_ref, v2_ref, bias_ref,
                        o_ref, kbuf, vbuf, oslab, lslab, btile):
    c = pl.program_id(2)
    key_col = lax.broadcasted_iota(jnp.int32, (1, 2 * Q_BLOCK), 1)

    @pl.when(c == 0)
    def _():
        for g in range(N_DIL_GROUPS):
            for hh in range(2):
                row = jnp.broadcast_to(bias_ref[g, 0, hh:hh + 1, :], (Q_BLOCK, 2 * Q_BLOCK))
                tile = pltpu.roll(row, 0, 1, stride=1, stride_axis=0)
                btile[g, hh, 0] = tile
                btile[g, hh, 1] = jnp.where(key_col < Q_BLOCK, NEG, tile)

    slot = c % 2
    cur_base = pl.multiple_of(slot * ATTN_CHUNK, ATTN_CHUNK)
    prev_base = pl.multiple_of((1 - slot) * ATTN_CHUNK, ATTN_CHUNK)
    q_refs = (q0_ref, q1_ref, q2_ref)
    k_refs = (k0_ref, k1_ref, k2_ref)
    v_refs = (v0_ref, v1_ref, v2_ref)

    @pl.when(c == 0)
    def _():
        zeros = jnp.zeros((ATTN_CHUNK, LANES), F32)
        for g in range(N_DIL_GROUPS):
            kbuf[g, pl.ds(prev_base, ATTN_CHUNK), :] = zeros
            vbuf[g, pl.ds(prev_base, ATTN_CHUNK), :] = zeros

    for g in range(N_DIL_GROUPS):
        kbuf[g, pl.ds(cur_base, ATTN_CHUNK), :] = k_refs[g][0]
        vbuf[g, pl.ds(cur_base, ATTN_CHUNK), :] = v_refs[g][0]

    lane = lax.broadcasted_iota(jnp.int32, (1, LANES), 1)
    first_head = lane < HEAD_DIM

    def strided(start, dil):
        return pl.ds(start, Q_BLOCK) if dil == 1 else pl.ds(start, Q_BLOCK, stride=dil)

    for g, (_, dil) in enumerate(DIL_PATTERNS):
        sub_blocks = ATTN_CHUNK // (Q_BLOCK * dil)
        span = Q_BLOCK * dil

        def block(blk, carry, g=g, dil=dil, sub_blocks=sub_blocks, span=span):
            res = blk // sub_blocks
            sb = blk % sub_blocks
            start = res + sb * span
            prev_start = jnp.where(sb > 0, cur_base + start - span,
                                   prev_base + res + (sub_blocks - 1) * span)
            qb = q_refs[g][0, strided(start, dil), :].astype(BF16)
            kk = jnp.concatenate([kbuf[g, strided(prev_start, dil), :],
                                  kbuf[g, strided(cur_base + start, dil), :]], axis=0).astype(BF16)
            vv = jnp.concatenate([vbuf[g, strided(prev_start, dil), :],
                                  vbuf[g, strided(cur_base + start, dil), :]], axis=0).astype(BF16)
            no_prev = jnp.logical_and(c == 0, sb == 0).astype(jnp.int32)
            outs, lses = [], []
            for hh in range(2):
                head_lanes = first_head if hh == 0 else jnp.logical_not(first_head)
                qh = jnp.where(head_lanes, qb, jnp.zeros_like(qb))
                s = lax.dot_general(qh, kk, (((1,), (1,)), ((), ())), preferred_element_type=F32)
                s = s + btile[g, hh, no_prev]
                m = jnp.max(s, axis=1, keepdims=True)
                p = jnp.exp(s - m)
                l = jnp.sum(p, axis=1, keepdims=True)
                pv = _dot(p.astype(BF16), vv)
                outs.append(pv * (1.0 / l))
                lses.append(m + jnp.log(l))
            oslab[g, strided(start, dil), :] = jnp.where(first_head, outs[0], outs[1])
            lslab[g, strided(start, dil), :] = jnp.where(first_head, lses[0], lses[1])
            return carry

        lax.fori_loop(0, ATTN_CHUNK // Q_BLOCK, block, 0, unroll=BLOCK_UNROLL)

    def combine(i, carry):
        rows = pl.ds(pl.multiple_of(i * MXU_DIM, MXU_DIM), MXU_DIM)
        ls = [lslab[g, rows, :] for g in range(N_DIL_GROUPS)]
        top = jnp.maximum(jnp.maximum(ls[0], ls[1]), ls[2])
        es = [jnp.exp(l - top) for l in ls]
        den = es[0] + es[1] + es[2]
        num = es[0] * oslab[0, rows, :] + es[1] * oslab[1, rows, :] + es[2] * oslab[2, rows, :]
        o_ref[0, rows, :] = num / den
        return carry

    lax.fori_loop(0, ATTN_CHUNK // MXU_DIM, combine, 0)


def _group_bias(rel_bias, g):
    buckets = _bucket_table(np.arange(N_KEYS) * DIL_PATTERNS[g][1])
    return rel_bias[buckets][:, g * HEADS_PER_GROUP:(g + 1) * HEADS_PER_GROUP]


def _prompt_bias_rows(rel_bias):
    rows = []
    for g in range(N_DIL_GROUPS):
        by_key = _group_bias(rel_bias, g)[::-1].T
        rows.append(jnp.pad(by_key, ((0, 0), (0, 2 * Q_BLOCK - N_KEYS)), constant_values=NEG))
    return jnp.stack(rows).reshape(N_DIL_GROUPS, HEADS_PER_GROUP // 2, 2, 2 * Q_BLOCK).astype(F32)


def _attn_prompt(q, kv, bias_rows):
    batch, seq, _ = q.shape
    chunks = seq // ATTN_CHUNK
    pairs = ATTN_GROUP_WIDTH // LANES
    blk = (1, ATTN_CHUNK, LANES)
    q_specs = [pl.BlockSpec(blk, functools.partial(lambda b, hp, c, g: (b, c, g * pairs + hp), g=g))
               for g in range(N_DIL_GROUPS)]
    k_specs = [pl.BlockSpec(blk, functools.partial(lambda b, hp, c, g: (b, c, 2 * g * pairs + hp), g=g))
               for g in range(N_DIL_GROUPS)]
    v_specs = [pl.BlockSpec(blk, functools.partial(lambda b, hp, c, g: (b, c, (2 * g + 1) * pairs + hp), g=g))
               for g in range(N_DIL_GROUPS)]
    bias_spec = pl.BlockSpec((N_DIL_GROUPS, 1, 2, 2 * Q_BLOCK), lambda b, hp, c: (0, hp, 0, 0))
    return pl.pallas_call(
        _attn_prompt_kernel, grid=(batch, pairs, chunks),
        in_specs=q_specs + k_specs + v_specs + [bias_spec],
        out_specs=pl.BlockSpec(blk, lambda b, hp, c: (b, c, hp)),
        out_shape=jax.ShapeDtypeStruct((batch, seq, ATTN_GROUP_WIDTH), F32),
        scratch_shapes=[pltpu.VMEM((N_DIL_GROUPS, 2 * ATTN_CHUNK, LANES), F32),
                        pltpu.VMEM((N_DIL_GROUPS, 2 * ATTN_CHUNK, LANES), F32),
                        pltpu.VMEM((N_DIL_GROUPS, ATTN_CHUNK, LANES), F32),
                        pltpu.VMEM((N_DIL_GROUPS, ATTN_CHUNK, LANES), F32),
                        pltpu.VMEM((N_DIL_GROUPS, 2, 2, Q_BLOCK, 2 * Q_BLOCK), F32)],
        compiler_params=pltpu.CompilerParams(dimension_semantics=("arbitrary", "arbitrary", "arbitrary"),
                                             vmem_limit_bytes=VMEM_LIMIT_BYTES),
        name="attn_prompt")(q, q, q, kv, kv, kv, kv, kv, kv, bias_rows)


def _expand_heads(x):
    width = x.shape[-1]
    return jnp.broadcast_to(x[:, None, :], (HEADS_PER_GROUP, HEAD_DIM, width)).reshape(ATTN_GROUP_WIDTH, width)


def _head_sums(x):
    return x.reshape(HEADS_PER_GROUP, HEAD_DIM, x.shape[-1]).sum(axis=1)


def _softmax_parts(s):
    m = jnp.max(s, axis=1, keepdims=True)
    p = jnp.exp(s - m)
    return m, p, jnp.sum(p, axis=1, keepdims=True)


def _attn_sample_kernel(q_ref, kv_ref, c0_ref, c1_ref, c2_ref, b0_ref, b1_ref, b2_ref, o_ref, *, steps):
    lane = lax.broadcasted_iota(jnp.int32, (1, LANES), 1)
    sub = lax.broadcasted_iota(jnp.int32, (LANES, 1), 0)
    group_cols = KV_WIDTH // N_DIL_GROUPS

    def lanes_from_rows(mat, period):
        res = sub & (period - 1)
        rows = jnp.zeros((LANES, ATTN_GROUP_WIDTH), F32)
        for i in range(steps):
            rows = jnp.where(res == i, mat[i:i + 1, :], rows)
        return rows.T

    def new_rows(g, which, period):
        lo = g * group_cols + which * ATTN_GROUP_WIDTH
        return lanes_from_rows(kv_ref[0, :, lo:lo + ATTN_GROUP_WIDTH], period)

    outs = [[None] * N_DIL_GROUPS for _ in range(steps)]
    stats = [[None] * N_DIL_GROUPS for _ in range(steps)]

    q0 = q_ref[0, :, 0:ATTN_GROUP_WIDTH]
    kn, vn = new_rows(0, 0, LANES), new_rows(0, 1, LANES)
    for i in range(steps):
        qpat = jnp.broadcast_to(q0[i:i + 1, :], (LANES, ATTN_GROUP_WIDTH)).T
        s = jnp.concatenate([_head_sums(c0_ref[0, 0] * qpat), _head_sums(kn * qpat)], axis=1) + b0_ref[i]
        m, p, l = _softmax_parts(s)
        acc = c0_ref[0, 1] * _expand_heads(p[:, :LANES]) + vn * _expand_heads(p[:, LANES:])
        outs[i][0] = jnp.sum(acc, axis=1, keepdims=True)
        stats[i][0] = (m, l)

    for g, c_ref, b_ref in ((1, c1_ref, b1_ref), (2, c2_ref, b2_ref)):
        win, dil = DIL_PATTERNS[g]
        tiles = win // LANES
        qpat = lanes_from_rows(q_ref[0, :, g * ATTN_GROUP_WIDTH:(g + 1) * ATTN_GROUP_WIDTH], dil)
        kn, vn = new_rows(g, 0, dil), new_rows(g, 1, dil)
        parts = [_head_sums(c_ref[0, 0, :, t * LANES:(t + 1) * LANES] * qpat) for t in range(tiles)]
        s_all = jnp.concatenate(parts + [_head_sums(kn * qpat)], axis=1) + b_ref[...]
        res_all = lax.broadcasted_iota(jnp.int32, (1, win + LANES), 1) & (dil - 1)
        p_all = jnp.zeros_like(s_all)
        for i in range(steps):
            m, p, l = _softmax_parts(jnp.where(res_all == i, s_all, NEG))
            p_all = p_all + p
            stats[i][g] = (m, l)
        acc = vn * _expand_heads(p_all[:, win:])
        for t in range(tiles):
            acc = acc + c_ref[0, 1, :, t * LANES:(t + 1) * LANES] * _expand_heads(p_all[:, t * LANES:(t + 1) * LANES])
        res = lane & (dil - 1)
        for i in range(steps):
            outs[i][g] = jnp.sum(jnp.where(res == i, acc, 0.0), axis=1, keepdims=True)

    cols = jnp.zeros((ATTN_GROUP_WIDTH, LANES), F32)
    for i in range(steps):
        lses = [m + jnp.log(l) for m, l in stats[i]]
        top = jnp.maximum(jnp.maximum(lses[0], lses[1]), lses[2])
        es = [jnp.exp(x - top) for x in lses]
        den = es[0] + es[1] + es[2]
        col = sum(outs[i][g] * _expand_heads(es[g] / stats[i][g][1]) for g in range(N_DIL_GROUPS))
        cols = jnp.where(lane == i, col / _expand_heads(den), cols)
    o_ref[0] = cols.T[0:steps, :]


def _sample_bias(rel_bias, steps):
    by_dist = [_group_bias(rel_bias, g).T for g in range(N_DIL_GROUPS)]
    win0 = DIL_PATTERNS[0][0]
    far_to_near = by_dist[0][:, ::-1]
    per_query = []
    for i in range(steps):
        cache = jnp.pad(far_to_near[:, :win0 - i], ((0, 0), (i, LANES - win0)), constant_values=NEG)
        new = jnp.pad(by_dist[0][:, :i + 1][:, ::-1], ((0, 0), (0, LANES - 1 - i)), constant_values=NEG)
        per_query.append(jnp.concatenate([cache, new], axis=1))
    tables = [jnp.stack(per_query).astype(F32)]
    for g in (1, 2):
        win, dil = DIL_PATTERNS[g]
        cache = jnp.repeat(by_dist[g][:, :0:-1], dil, axis=1)
        cache = jnp.where((np.arange(win) % dil < steps)[None], cache, NEG)
        new = jnp.pad(jnp.broadcast_to(by_dist[g][:, 0:1], (HEADS_PER_GROUP, steps)),
                      ((0, 0), (0, LANES - steps)), constant_values=NEG)
        tables.append(jnp.concatenate([cache, new], axis=1).astype(F32))
    return tables


def _attn_sample(q, kvn, caches, bias_tables):
    batch, steps, _ = q.shape
    views, specs = [], []
    for g, (win, dil) in enumerate(DIL_PATTERNS):
        assert caches[g].shape[1] == win and win % LANES == 0 and (g == 0 or steps <= dil)
        views.append(jnp.transpose(caches[g], (0, 2, 3, 4, 1)).reshape(batch, 2, ATTN_GROUP_WIDTH, win))
        specs.append(pl.BlockSpec((1, 2, ATTN_GROUP_WIDTH, win), lambda b: (b, 0, 0, 0)))
    whole = lambda a: pl.BlockSpec(a.shape, lambda b: (0,) * a.ndim)
    return pl.pallas_call(
        functools.partial(_attn_sample_kernel, steps=steps), grid=(batch,),
        in_specs=[pl.BlockSpec((1,) + q.shape[1:], lambda b: (b, 0, 0)),
                  pl.BlockSpec((1,) + kvn.shape[1:], lambda b: (b, 0, 0))] + specs
                 + [whole(t) for t in bias_tables],
        out_specs=pl.BlockSpec((1, steps, ATTN_GROUP_WIDTH), lambda b: (b, 0, 0)),
        out_shape=jax.ShapeDtypeStruct((batch, steps, ATTN_GROUP_WIDTH), F32),
        compiler_params=pltpu.CompilerParams(dimension_semantics=("arbitrary",),
                                             vmem_limit_bytes=VMEM_LIMIT_BYTES),
        name="attn_sample")(q, kvn, *views, *bias_tables)


def kernel(x_prompt, x_sample, state_pool, cache_kv_w128, cache_kv_w512, cache_kv_w2048, ln_g, ln_b, ffn_w13,
           ffn_w2, pool_w_in, pool_w_grp, pool_scale, pool_w_out, attn_w_kv, attn_w_q, attn_w_o, rel_bias):
    batch, seq, _ = x_prompt.shape
    dec_batch, steps, _ = x_sample.shape
    assert DEPTH == 2 and state_pool.shape[0] == 1 and attn_w_q.shape[0] == 1
    assert seq % ATTN_CHUNK == 0 and seq >= DIL_PATTERNS[-1][0]

    scale = pool_scale[0][None]
    ln = lambda layer, k: (ln_g[layer, k][None], ln_b[layer, k][None])
    q_scale = HEAD_DIM ** -0.5
    grp_rows = len(POOL_WINDOWS) * POOL_GROUP_DIM

    w13_00 = ffn_w13[0, 0].astype(BF16)
    w2_00 = ffn_w2[0, 0].astype(BF16)

    xp = x_prompt.reshape(batch * seq, D_MODEL)
    xp, w13_01, w2_01, w_kv, w_in, w_grp, w_out = _ffn(
        xp, w13_00, w2_00, *ln(0, 0),
        cast=((ffn_w13, (0, 1)), (ffn_w2, (0, 1)), (attn_w_kv, ()), (pool_w_in, (0,)),
              (pool_w_grp.reshape(1, grp_rows, POOL_GROUP_DIM), (0,)), (pool_w_out, (0,))))
    w_grp = w_grp.reshape(len(POOL_WINDOWS), POOL_GROUP_DIM, POOL_GROUP_DIM)
    xp, pool_p = _pool_prompt(xp, batch, w_in, w_grp, scale, w_out, *ln(0, 1))
    xp, kv_p, w13_10, w2_10, w_q = _ffn(
        xp, w13_01, w2_01, *ln(0, 2), wp=w_kv,
        cast=((ffn_w13, (1, 0)), (ffn_w2, (1, 0)), (attn_w_q, (0,))))
    xp, q_p, w13_11, w2_11, w_o = _ffn(
        xp, w13_10, w2_10, *ln(1, 0), wp=w_q, proj_scale=q_scale,
        cast=((ffn_w13, (1, 1)), (ffn_w2, (1, 1)), (attn_w_o, (0,))))

    def layer1_tail(x, attn):
        return _oproj_ffn(x, attn, w_o, *ln(1, 1), w13_11, w2_11, *ln(1, 2))

    kv_p = kv_p.reshape(batch, seq, KV_WIDTH)
    attn_p = _attn_prompt(q_p.reshape(batch, seq, Q_WIDTH), kv_p, _prompt_bias_rows(rel_bias))
    y_prompt = layer1_tail(xp, attn_p.reshape(batch * seq, ATTN_GROUP_WIDTH)).reshape(batch, seq, D_MODEL)
    pool_p = pool_p[:, CARRY_ROWS - POOL_PAD:][None]
    group_cols = KV_WIDTH // N_DIL_GROUPS
    kv_p_groups = [kv_p[:, seq - w:, g * group_cols:(g + 1) * group_cols]
                   .reshape(batch, w, 2, HEADS_PER_GROUP, HEAD_DIM) for g, (w, _) in enumerate(DIL_PATTERNS)]

    xs = jnp.swapaxes(x_sample, 0, 1).reshape(steps * dec_batch, D_MODEL)
    xs = _ffn(xs, w13_00, w2_00, *ln(0, 0))
    xs, pool_s = _pool_sample(xs, jnp.swapaxes(state_pool[0], 0, 1), steps, dec_batch,
                              w_in, w_grp, scale, w_out, *ln(0, 1))
    xs, kv_s = _ffn(xs, w13_01, w2_01, *ln(0, 2), wp=w_kv)
    xs, q_s = _ffn(xs, w13_10, w2_10, *ln(1, 0), wp=w_q, proj_scale=q_scale)
    kv_s = jnp.swapaxes(kv_s.reshape(steps, dec_batch, KV_WIDTH), 0, 1)
    q_s = jnp.swapaxes(q_s.reshape(steps, dec_batch, Q_WIDTH), 0, 1)
    attn_s = _attn_sample(q_s, kv_s, (cache_kv_w128, cache_kv_w512, cache_kv_w2048),
                          _sample_bias(rel_bias, steps))
    attn_s = jnp.swapaxes(attn_s, 0, 1).reshape(steps * dec_batch, ATTN_GROUP_WIDTH)
    y_sample = jnp.swapaxes(layer1_tail(xs, attn_s).reshape(steps, dec_batch, D_MODEL), 0, 1)
    pool_s = jnp.swapaxes(pool_s, 0, 1)[None]
    kv_s = kv_s.reshape(dec_batch, steps, N_DIL_GROUPS, 2, HEADS_PER_GROUP, HEAD_DIM)

    return (y_prompt, y_sample, pool_p, pool_s, kv_p_groups[0], kv_p_groups[1], kv_p_groups[2],
            kv_s[:, :, 0], kv_s[:, :, 1], kv_s[:, :, 2])
```

```python
import functools

import numpy as np
import jax
import jax.numpy as jnp
from jax import lax
from jax.experimental import pallas as pl
from jax.experimental.pallas import tpu as pltpu

F32 = jnp.float32
BF16 = jnp.bfloat16

D_MODEL = 1024
DEPTH = 2
PAST_LEN = 16384
POOL_WINDOWS = (2, 4, 8, 16)
POOL_GROUP_DIM = D_MODEL // len(POOL_WINDOWS)
POOL_PAD = max(POOL_WINDOWS) - 1
DIL_PATTERNS = ((128, 1), (512, 4), (2048, 16))
N_DIL_GROUPS = len(DIL_PATTERNS)
HEAD_DIM = 64
HEADS_PER_GROUP = 8
ATTN_GROUP_WIDTH = HEADS_PER_GROUP * HEAD_DIM
Q_WIDTH = N_DIL_GROUPS * ATTN_GROUP_WIDTH
KV_WIDTH = 2 * Q_WIDTH
N_BUCKETS = 32
MAX_DISTANCE = 2048
D_FF = 2816
ALPHA = (2.0 * DEPTH) ** 0.25
LN_EPS = 1e-5
NEG = -1e30
LOG2E = 1.4426950408889634
N_KEYS = 129

LANES = 128
MXU_DIM = 256
VMEM_LIMIT_BYTES = 56 * 1024 * 1024
VMEM_LIMIT_FUSED_BYTES = 60 * 1024 * 1024

FF_CHUNK = MXU_DIM
ROW_TILE = 512
Q_BLOCK = 128
ATTN_CHUNK = 2048


def _bucket_table(dist):
    dist = np.asarray(dist, dtype=np.int64)
    max_exact = N_BUCKETS // 2
    ratio = np.maximum(dist, 1).astype(np.float64) / max_exact
    large = max_exact + (np.log(ratio) / np.log(MAX_DISTANCE / max_exact) * (N_BUCKETS - max_exact)).astype(np.int64)
    large = np.minimum(large, N_BUCKETS - 1)
    return np.where(dist < max_exact, dist, large).astype(np.int32)


def _layer_norm(z, g, b):
    mu = jnp.mean(z, axis=-1, keepdims=True)
    zc = z - mu
    var = jnp.mean(zc * zc, axis=-1, keepdims=True)
    return zc * lax.rsqrt(var + LN_EPS) * g + b


def _dot(a, b):
    return jnp.dot(a, b, preferred_element_type=F32)


def _swiglu(xb, w13_ref, w2_ref):
    acc = None
    for j in range(D_FF // FF_CHUNK):
        lo = j * FF_CHUNK
        gate = _dot(xb, w13_ref[:, lo:lo + FF_CHUNK].astype(BF16))
        up = _dot(xb, w13_ref[:, D_FF + lo:D_FF + lo + FF_CHUNK].astype(BF16))
        act = (gate * jax.nn.sigmoid(gate) * up).astype(BF16)
        y = _dot(act, w2_ref[lo:lo + FF_CHUNK, :].astype(BF16))
        acc = y if acc is None else acc + y
    return acc


def _ffn_kernel(*refs, proj_split, proj_scales, n_cast, prompt_steps):
    has_proj = proj_split > 0
    n_in = 6 + has_proj + n_cast
    x_refs = refs[:2]
    w13_ref, w2_ref, g_ref, b_ref = refs[2:6]
    o_refs = refs[n_in:n_in + 2]
    p_refs = refs[n_in + 2:n_in + 2 + proj_split]
    n_out = 2 + (proj_split + 1 if has_proj else 0)

    def run(x_ref, o_ref, p_refs, proj_scale):
        x = x_ref[...]
        y = _swiglu(x.astype(BF16), w13_ref, w2_ref)
        out = _layer_norm(ALPHA * x + 0.5 * y, g_ref[...], b_ref[...])
        o_ref[...] = out
        if has_proj:
            p = _dot(out.astype(BF16), refs[6][...]) * proj_scale
            width = p.shape[1] // len(p_refs)
            for j, p_ref in enumerate(p_refs):
                p_ref[...] = p[:, j * width:(j + 1) * width]

    step = pl.program_id(0)
    pl.when(step < prompt_steps)(lambda: run(x_refs[0], o_refs[0], p_refs, proj_scales[0]))
    pl.when(step == prompt_steps)(
        lambda: run(x_refs[1], o_refs[1], refs[n_in + 2 + proj_split:n_in + n_out], proj_scales[1]))
    for src, dst in zip(refs[6 + has_proj:n_in], refs[n_in + n_out:]):
        dst[...] = src[...].astype(BF16)


def _resident(shape):
    nd = len(shape)
    return pl.BlockSpec(shape, lambda *_: (0,) * nd, pipeline_mode=pl.Buffered(1))


BF16_SUBLANES = 16


def _cast_chunks(n_rows, steps):
    chunks = steps
    while n_rows % chunks or (n_rows // chunks) % BF16_SUBLANES:
        chunks //= 2
    return chunks


def _whole(shape):
    nd = len(shape)
    return pl.BlockSpec(shape, lambda *_: (0,) * nd)


def _ffn(x, xs, w13, w2, g, b, wp=None, proj_scales=(1.0, 1.0), proj_split=1, cast=()):
    rows = x.shape[0]
    tm = ROW_TILE
    steps = rows // tm
    assert rows % tm == 0
    tile_index = lambda i: (jnp.minimum(i, steps - 1), 0)
    row_spec = pl.BlockSpec((tm, D_MODEL), tile_index)
    operands = [x, xs, w13, w2, g, b] + ([] if wp is None else [wp])
    in_specs = [row_spec] + [None if isinstance(a, tuple) else _resident(a.shape) for a in operands[1:]]
    for k in (2, 3):
        if isinstance(operands[k], tuple):
            operands[k], lead = operands[k]
            in_specs[k] = pl.BlockSpec((None,) * len(lead) + operands[k].shape[len(lead):],
                                       functools.partial(lambda i, lead: lead + (0, 0), lead=lead),
                                       pipeline_mode=pl.Buffered(1))
    out_specs = [row_spec, _whole(xs.shape)]
    out_shape = [jax.ShapeDtypeStruct((rows, D_MODEL), F32), jax.ShapeDtypeStruct(xs.shape, F32)]
    if wp is not None:
        assert wp.shape[1] % (proj_split * LANES) == 0
        width = wp.shape[1] // proj_split
        out_specs += [pl.BlockSpec((tm, width), tile_index)] * proj_split + [_whole((xs.shape[0], wp.shape[1]))]
        out_shape += ([jax.ShapeDtypeStruct((rows, width), F32)] * proj_split
                      + [jax.ShapeDtypeStruct((xs.shape[0], wp.shape[1]), F32)])
    for w, lead in cast:
        n_rows, n_cols = w.shape[len(lead):]
        chunks = _cast_chunks(n_rows, steps)
        chunk_index = functools.partial(lambda i, last: (jnp.minimum(i, last), 0), last=chunks - 1)
        in_specs.append(pl.BlockSpec((None,) * len(lead) + (n_rows // chunks, n_cols),
                                     functools.partial(lambda i, lead, f: lead + f(i), lead=lead, f=chunk_index)))
        out_specs.append(pl.BlockSpec((n_rows // chunks, n_cols), chunk_index))
        out_shape.append(jax.ShapeDtypeStruct((n_rows, n_cols), BF16))
    return pl.pallas_call(
        functools.partial(_ffn_kernel, proj_split=0 if wp is None else proj_split, proj_scales=proj_scales,
                          n_cast=len(cast), prompt_steps=steps),
        grid=(steps + 1,), in_specs=in_specs, out_specs=out_specs, out_shape=out_shape,
        compiler_params=pltpu.CompilerParams(dimension_semantics=("arbitrary",),
                                             vmem_limit_bytes=VMEM_LIMIT_BYTES),
        name="ffn")(*operands, *[w for w, _ in cast])


def _oproj_ffn_kernel(x_ref, a_ref, xs_ref, q_ref, kv_ref, c0_ref, c1_ref, c2_ref, b0_ref, b1_ref, b2_ref,
                      wo_ref, g1_ref, be1_ref, w13_ref, w2_ref, g2_ref, be2_ref, o_ref, os_ref, attn_s,
                      *, prompt_steps, sample_steps, sample_batch):
    def run(x_ref, attn, o_ref):
        mix = _dot(attn.astype(BF16), wo_ref[...])
        x = _layer_norm(ALPHA * x_ref[...] + mix, g1_ref[...], be1_ref[...])
        y = _swiglu(x.astype(BF16), w13_ref, w2_ref)
        o_ref[...] = _layer_norm(ALPHA * x + 0.5 * y, g2_ref[...], be2_ref[...])

    step = pl.program_id(0)

    @pl.when(step < prompt_steps)
    def _():
        run(x_ref, a_ref[...], o_ref)
        rows = _attn_sample_rows(q_ref, kv_ref, (c0_ref, c1_ref, c2_ref), (b0_ref, b1_ref, b2_ref), sample_steps)
        for t in range(sample_steps):
            attn_s[pl.ds(t * sample_batch + step, 1), :] = rows[t:t + 1, :]

    @pl.when(step == prompt_steps)
    def _():
        run(xs_ref, attn_s[...], os_ref)


def _oproj_ffn(x, a, xs, q_s, kv_s, caches, bias_tables, wo, g1, b1, w13, w2, g2, b2):
    rows = x.shape[0]
    tm = ROW_TILE
    steps = rows // tm
    batch, sample_steps, _ = q_s.shape
    assert rows % tm == 0 and batch == steps and xs.shape[0] == batch * sample_steps
    tile = lambda i: jnp.minimum(i, steps - 1)
    row_spec = pl.BlockSpec((tm, D_MODEL), lambda i: (tile(i), 0))
    views, cache_specs = [], []
    for g, (win, dil) in enumerate(DIL_PATTERNS):
        assert caches[g].shape[1] == win and win % LANES == 0 and (g == 0 or sample_steps <= dil)
        views.append(jnp.transpose(caches[g], (0, 2, 3, 4, 1)).reshape(batch, 2, ATTN_GROUP_WIDTH, win))
        cache_specs.append(pl.BlockSpec((1, 2, ATTN_GROUP_WIDTH, win), lambda i: (tile(i), 0, 0, 0)))
    weights = (wo, g1, b1, w13, w2, g2, b2)
    return pl.pallas_call(
        functools.partial(_oproj_ffn_kernel, prompt_steps=steps, sample_steps=sample_steps, sample_batch=batch),
        grid=(steps + 1,),
        in_specs=[row_spec, pl.BlockSpec((tm, ATTN_GROUP_WIDTH), lambda i: (tile(i), 0)), _resident(xs.shape),
                  pl.BlockSpec((1,) + q_s.shape[1:], lambda i: (tile(i), 0, 0)),
                  pl.BlockSpec((1,) + kv_s.shape[1:], lambda i: (tile(i), 0, 0))]
                 + cache_specs + [_resident(t.shape) for t in bias_tables] + [_resident(w.shape) for w in weights],
        out_specs=[row_spec, _whole(xs.shape)],
        out_shape=[jax.ShapeDtypeStruct((rows, D_MODEL), F32), jax.ShapeDtypeStruct(xs.shape, F32)],
        scratch_shapes=[pltpu.VMEM((xs.shape[0], ATTN_GROUP_WIDTH), F32)],
        compiler_params=pltpu.CompilerParams(dimension_semantics=("arbitrary",),
                                             vmem_limit_bytes=VMEM_LIMIT_FUSED_BYTES),
        name="oproj_ffn")(x, a, xs, q_s, kv_s, *views, *bias_tables, *weights)


CARRY_ROWS = 16
PAD_ROWS = 8
POOL_SUB_TILES = 2


def _pool_tail(x, u, pooled_groups, wgrp_ref, scale_ref, wout_ref, g_ref, b_ref):
    zs = [_dot(p.astype(BF16), wgrp_ref[gi]) for gi, p in enumerate(pooled_groups)]
    z = jnp.concatenate(zs, axis=-1) * scale_ref[...]
    mix = _dot(z.astype(BF16), wout_ref[...])
    return _layer_norm(ALPHA * x + mix, g_ref[...], b_ref[...])


def _pool_prompt_kernel(x_ref, win_ref, wgrp_ref, scale_ref, wout_ref, g_ref, b_ref, o_ref, st_ref,
                        ue_ref, s2_ref, s4_ref, s8_ref, *, tm, sub, tiles):
    t = pl.program_id(1)
    lo, base = PAD_ROWS, PAD_ROWS + CARRY_ROWS
    stages = (ue_ref, s2_ref, s4_ref, s8_ref)

    @pl.when(t == 0)
    def _():
        ue_ref[0:base, :] = jnp.zeros((base, D_MODEL), F32)
        for ref in stages[1:]:
            ref[0:lo, :] = jnp.zeros((lo, D_MODEL), F32)

    xs, us = [], []
    for first in range(0, tm, sub):
        xs.append(x_ref[first:first + sub, :])
        us.append(_dot(xs[-1].astype(BF16), win_ref[...]))
        ue_ref[base + first:base + first + sub, :] = us[-1]
    for j, first in enumerate(range(0, tm, sub)):
        r0, r1 = base + first, base + first + sub
        x, u = xs[j], us[j]
        pos = t * tm + first + lax.broadcasted_iota(jnp.int32, (sub, 1), 0)
        s0 = lo if first == 0 else r0
        pooled = []
        for gi, w in enumerate(POOL_WINDOWS):
            cols = slice(gi * POOL_GROUP_DIM, (gi + 1) * POOL_GROUP_DIM)
            n_stages = w.bit_length() - 1
            for level in range(n_stages - 1):
                src, dst, k = stages[level], stages[level + 1], 2 ** level
                dst[s0:r1, cols] = src[s0:r1, cols] + src[s0 - k:r1 - k, cols]
            src, k = stages[n_stages - 1], w // 2
            win_sum = src[r0:r1, cols] + src[r0 - k:r1 - k, cols]
            inv_count = 1.0 / jnp.minimum(pos + 1, w).astype(F32)
            pooled.append(win_sum * inv_count - u[:, cols])
        o_ref[first:first + sub, :] = _pool_tail(x, u, pooled, wgrp_ref, scale_ref, wout_ref, g_ref, b_ref)

    tail = ue_ref[base + tm - CARRY_ROWS:base + tm, :]

    @pl.when(t == tiles - 1)
    def _():
        st_ref[0] = tail

    ue_ref[lo:base, :] = tail


def _pool_prompt(x, batch, w_in, w_grp, scale, w_out, g, b):
    rows = x.shape[0]
    seq = rows // batch
    sub = ROW_TILE
    tm = POOL_SUB_TILES * sub
    assert seq % tm == 0
    tiles = seq // tm
    row_spec = pl.BlockSpec((tm, D_MODEL), lambda bi, t: (bi * tiles + t, 0))
    return pl.pallas_call(
        functools.partial(_pool_prompt_kernel, tm=tm, sub=sub, tiles=tiles), grid=(batch, tiles),
        in_specs=[row_spec, _resident(w_in.shape), _resident(w_grp.shape), _resident(scale.shape),
                  _resident(w_out.shape), _resident(g.shape), _resident(b.shape)],
        out_specs=[row_spec, pl.BlockSpec((1, CARRY_ROWS, D_MODEL), lambda bi, t: (bi, 0, 0))],
        out_shape=[jax.ShapeDtypeStruct((rows, D_MODEL), F32),
                   jax.ShapeDtypeStruct((batch, CARRY_ROWS, D_MODEL), F32)],
        scratch_shapes=[pltpu.VMEM((PAD_ROWS + CARRY_ROWS + tm, D_MODEL), F32)] * 4,
        compiler_params=pltpu.CompilerParams(dimension_semantics=("arbitrary", "arbitrary"),
                                             vmem_limit_bytes=VMEM_LIMIT_BYTES),
        name="pool_prompt")(x, w_in, w_grp, scale, w_out, g, b)


def _pool_sample_kernel(x_ref, st_ref, win_ref, wgrp_ref, scale_ref, wout_ref, g_ref, b_ref, o_ref, nst_ref,
                        *, steps, batch):
    x = x_ref[...]
    u = _dot(x.astype(BF16), win_ref[...])
    ext = [st_ref[k] for k in range(POOL_PAD)] + [u[i * batch:(i + 1) * batch, :] for i in range(steps)]
    pooled = []
    for gi, w in enumerate(POOL_WINDOWS):
        cols = slice(gi * POOL_GROUP_DIM, (gi + 1) * POOL_GROUP_DIM)
        per_step = []
        for i in range(steps):
            win_sum = ext[POOL_PAD + i][:, cols]
            for k in range(1, w):
                win_sum = win_sum + ext[POOL_PAD + i - k][:, cols]
            count = float(min(PAST_LEN + i + 1, w))
            per_step.append(win_sum / count - ext[POOL_PAD + i][:, cols])
        pooled.append(jnp.concatenate(per_step, axis=0))
    o_ref[...] = _pool_tail(x, u, pooled, wgrp_ref, scale_ref, wout_ref, g_ref, b_ref)
    for k in range(POOL_PAD):
        nst_ref[k] = ext[steps + k]


def _pool_sample(x, st, steps, batch, w_in, w_grp, scale, w_out, g, b):
    rows = x.shape[0]
    full = lambda shape: pl.BlockSpec(shape, lambda i: (0,) * len(shape))
    return pl.pallas_call(
        functools.partial(_pool_sample_kernel, steps=steps, batch=batch), grid=(1,),
        in_specs=[full(x.shape), full(st.shape), full(w_in.shape), full(w_grp.shape), full(scale.shape),
                  full(w_out.shape), full(g.shape), full(b.shape)],
        out_specs=[full(x.shape), full(st.shape)],
        out_shape=[jax.ShapeDtypeStruct((rows, D_MODEL), F32), jax.ShapeDtypeStruct(st.shape, F32)],
        compiler_params=pltpu.CompilerParams(dimension_semantics=("arbitrary",),
                                             vmem_limit_bytes=VMEM_LIMIT_BYTES),
        name="pool_sample")(x, st, w_in, w_grp, scale, w_out, g, b)


def _attn_prompt_kernel(*refs):
    n = N_DIL_GROUPS
    q_refs, k_refs, v_refs, kp_refs, vp_refs = (refs[i * n:(i + 1) * n] for i in range(5))
    bias_ref, o_ref = refs[5 * n:5 * n + 2]
    tail_refs = refs[5 * n + 2:6 * n + 2]
    oslab, mslab, lslab, btile = refs[6 * n + 2:]
    c = pl.program_id(2)
    first_chunk = (c == 0).astype(jnp.int32)
    key_col = lax.broadcasted_iota(jnp.int32, (1, 2 * Q_BLOCK), 1)

    @pl.when(c == 0)
    def _():
        for g in range(N_DIL_GROUPS):
            for hh in range(2):
                row = jnp.broadcast_to(bias_ref[g, 0, hh:hh + 1, :] * LOG2E, (Q_BLOCK, 2 * Q_BLOCK))
                tile = pltpu.roll(row, 0, 1, stride=1, stride_axis=0)
                btile[g, hh, 0] = tile
                btile[g, hh, 1] = jnp.where(key_col < Q_BLOCK, NEG, tile)

    lane = lax.broadcasted_iota(jnp.int32, (1, LANES), 1)
    first_head = lane < HEAD_DIM
    ones = jnp.ones((2 * Q_BLOCK, LANES), BF16)

    assert DIL_PATTERNS[0][1] == 1
    for g in reversed(range(N_DIL_GROUPS)):
        dil = DIL_PATTERNS[g][1]
        sub_blocks = ATTN_CHUNK // (Q_BLOCK * dil)
        span = Q_BLOCK * dil

        def rows(res, sb, dil=dil, span=span):
            start = res + sb * span
            return pl.ds(start, Q_BLOCK) if dil == 1 else pl.ds(start, Q_BLOCK, stride=dil)

        for res in range(dil):
            k_prev = kp_refs[g][0, rows(res, 0), :].astype(BF16)
            v_prev = vp_refs[g][0, rows(res, 0), :].astype(BF16)
            for sb in range(sub_blocks):
                k_cur = k_refs[g][0, rows(res, sb), :].astype(BF16)
                v_cur = v_refs[g][0, rows(res, sb), :].astype(BF16)
                kk = jnp.concatenate([k_prev, k_cur], axis=0)
                vv = jnp.concatenate([jnp.concatenate([v_prev, v_cur], axis=0), ones], axis=1)
                qb = q_refs[g][0, rows(res, sb), :].astype(BF16)
                outs, maxes, sums = [], [], []
                for hh in range(2):
                    head_lanes = first_head if hh == 0 else jnp.logical_not(first_head)
                    qh = jnp.where(head_lanes, qb, jnp.zeros_like(qb))
                    s = lax.dot_general(qh, kk, (((1,), (1,)), ((), ())), preferred_element_type=F32)
                    s = s + (btile[g, hh, first_chunk] if sb == 0 else btile[g, hh, 0])
                    m = jnp.max(s, axis=1, keepdims=True)
                    pv = _dot(jnp.exp2(s - m).astype(BF16), vv)
                    outs.append(pv[:, :LANES])
                    maxes.append(m)
                    sums.append(pv[:, LANES:])
                out = jnp.where(first_head, outs[0], outs[1])
                row_max = jnp.where(first_head, maxes[0], maxes[1])
                row_sum = jnp.where(first_head, sums[0], sums[1])
                k_prev, v_prev = k_cur, v_cur
                if g > 0:
                    oslab[g - 1, rows(res, sb), :] = out
                    mslab[g - 1, rows(res, sb), :] = row_max
                    lslab[g - 1, rows(res, sb), :] = row_sum
                    continue
                blk = rows(res, sb)
                ms = [row_max] + [mslab[j, blk, :] for j in range(N_DIL_GROUPS - 1)]
                top = functools.reduce(jnp.maximum, ms)
                es = [jnp.exp2(m - top) for m in ms]
                den, num = es[0] * row_sum, es[0] * out
                for j in range(N_DIL_GROUPS - 1):
                    den = den + es[j + 1] * lslab[j, blk, :]
                    num = num + es[j + 1] * oslab[j, blk, :]
                o_ref[0, blk, :] = (num / den).astype(o_ref.dtype)

    @pl.when(c == pl.num_programs(2) - 1)
    def _():
        for g, (win, _) in enumerate(DIL_PATTERNS):
            tail_refs[g][0, 0] = k_refs[g][0, ATTN_CHUNK - win:, :].T
            tail_refs[g][0, 1] = v_refs[g][0, ATTN_CHUNK - win:, :].T


def _group_bias(rel_bias, g):
    buckets = _bucket_table(np.arange(N_KEYS) * DIL_PATTERNS[g][1])
    return rel_bias[buckets][:, g * HEADS_PER_GROUP:(g + 1) * HEADS_PER_GROUP]


def _prompt_bias_rows(rel_bias):
    rows = []
    for g in range(N_DIL_GROUPS):
        by_key = _group_bias(rel_bias, g)[::-1].T
        rows.append(jnp.pad(by_key, ((0, 0), (0, 2 * Q_BLOCK - N_KEYS)), constant_values=NEG))
    return jnp.stack(rows).reshape(N_DIL_GROUPS, HEADS_PER_GROUP // 2, 2, 2 * Q_BLOCK).astype(F32)


def _attn_prompt(q, kv, bias_rows):
    batch, seq, _ = q.shape
    chunks = seq // ATTN_CHUNK
    pairs = ATTN_GROUP_WIDTH // LANES
    blk = (1, ATTN_CHUNK, LANES)

    def spec(col_block):
        return pl.BlockSpec(blk, lambda b, hp, c: (b, c, col_block * pairs + hp))

    def prev_spec(col_block, dil):
        spans = ATTN_CHUNK // (Q_BLOCK * dil)
        return pl.BlockSpec((1, Q_BLOCK * dil, LANES),
                            lambda b, hp, c: (b, jnp.maximum(c * spans - 1, 0), col_block * pairs + hp))

    groups = range(N_DIL_GROUPS)
    dils = [dil for _, dil in DIL_PATTERNS]
    in_specs = ([spec(g) for g in groups] + [spec(0)] * N_DIL_GROUPS + [spec(1)] * N_DIL_GROUPS
                + [prev_spec(0, dils[g]) for g in groups] + [prev_spec(1, dils[g]) for g in groups]
                + [pl.BlockSpec((N_DIL_GROUPS, 1, 2, 2 * Q_BLOCK), lambda b, hp, c: (0, hp, 0, 0))])
    return pl.pallas_call(
        _attn_prompt_kernel, grid=(batch, pairs, chunks), in_specs=in_specs,
        out_specs=[pl.BlockSpec(blk, lambda b, hp, c: (b, c, hp))]
                  + [pl.BlockSpec((1, 2, LANES, win), lambda b, hp, c: (b, 0, hp, 0)) for win, _ in DIL_PATTERNS],
        out_shape=[jax.ShapeDtypeStruct((batch, seq, ATTN_GROUP_WIDTH), BF16)]
                  + [jax.ShapeDtypeStruct((batch, 2, ATTN_GROUP_WIDTH, win), F32) for win, _ in DIL_PATTERNS],
        scratch_shapes=[pltpu.VMEM((N_DIL_GROUPS - 1, ATTN_CHUNK, LANES), F32)] * 3
                       + [pltpu.VMEM((N_DIL_GROUPS, 2, 2, Q_BLOCK, 2 * Q_BLOCK), F32)],
        compiler_params=pltpu.CompilerParams(dimension_semantics=("arbitrary", "arbitrary", "arbitrary"),
                                             vmem_limit_bytes=VMEM_LIMIT_BYTES),
        name="attn_prompt")(*([q] * N_DIL_GROUPS + list(kv) * 4), bias_rows)


def _expand_heads(x):
    width = x.shape[-1]
    return jnp.broadcast_to(x[:, None, :], (HEADS_PER_GROUP, HEAD_DIM, width)).reshape(ATTN_GROUP_WIDTH, width)


def _head_sums(x):
    return x.reshape(HEADS_PER_GROUP, HEAD_DIM, x.shape[-1]).sum(axis=1)


def _softmax_parts(s):
    m = jnp.max(s, axis=1, keepdims=True)
    p = jnp.exp(s - m)
    return m, p, jnp.sum(p, axis=1, keepdims=True)


def _attn_sample_rows(q_ref, kv_ref, cache_refs, bias_refs, steps):
    c0_ref, c1_ref, c2_ref = cache_refs
    b0_ref, b1_ref, b2_ref = bias_refs
    lane = lax.broadcasted_iota(jnp.int32, (1, LANES), 1)
    sub = lax.broadcasted_iota(jnp.int32, (LANES, 1), 0)
    group_cols = KV_WIDTH // N_DIL_GROUPS

    def lanes_from_rows(mat, period):
        res = sub & (period - 1)
        rows = jnp.zeros((LANES, ATTN_GROUP_WIDTH), F32)
        for i in range(steps):
            rows = jnp.where(res == i, mat[i:i + 1, :], rows)
        return rows.T

    def new_rows(g, which, period):
        lo = g * group_cols + which * ATTN_GROUP_WIDTH
        return lanes_from_rows(kv_ref[0, :, lo:lo + ATTN_GROUP_WIDTH], period)

    outs = [[None] * N_DIL_GROUPS for _ in range(steps)]
    stats = [[None] * N_DIL_GROUPS for _ in range(steps)]

    q0 = q_ref[0, :, 0:ATTN_GROUP_WIDTH]
    kn, vn = new_rows(0, 0, LANES), new_rows(0, 1, LANES)
    for i in range(steps):
        qpat = jnp.broadcast_to(q0[i:i + 1, :], (LANES, ATTN_GROUP_WIDTH)).T
        s = jnp.concatenate([_head_sums(c0_ref[0, 0] * qpat), _head_sums(kn * qpat)], axis=1) + b0_ref[i]
        m, p, l = _softmax_parts(s)
        acc = c0_ref[0, 1] * _expand_heads(p[:, :LANES]) + vn * _expand_heads(p[:, LANES:])
        outs[i][0] = jnp.sum(acc, axis=1, keepdims=True)
        stats[i][0] = (m, l)

    for g, c_ref, b_ref in ((1, c1_ref, b1_ref), (2, c2_ref, b2_ref)):
        win, dil = DIL_PATTERNS[g]
        tiles = win // LANES
        qpat = lanes_from_rows(q_ref[0, :, g * ATTN_GROUP_WIDTH:(g + 1) * ATTN_GROUP_WIDTH], dil)
        kn, vn = new_rows(g, 0, dil), new_rows(g, 1, dil)
        parts = [_head_sums(c_ref[0, 0, :, t * LANES:(t + 1) * LANES] * qpat) for t in range(tiles)]
        s_all = jnp.concatenate(parts + [_head_sums(kn * qpat)], axis=1) + b_ref[...]
        res_all = lax.broadcasted_iota(jnp.int32, (1, win + LANES), 1) & (dil - 1)
        p_all = jnp.zeros_like(s_all)
        for i in range(steps):
            m, p, l = _softmax_parts(jnp.where(res_all == i, s_all, NEG))
            p_all = p_all + p
            stats[i][g] = (m, l)
        acc = vn * _expand_heads(p_all[:, win:])
        for t in range(tiles):
            acc = acc + c_ref[0, 1, :, t * LANES:(t + 1) * LANES] * _expand_heads(p_all[:, t * LANES:(t + 1) * LANES])
        res = lane & (dil - 1)
        for i in range(steps):
            outs[i][g] = jnp.sum(jnp.where(res == i, acc, 0.0), axis=1, keepdims=True)

    cols = jnp.zeros((ATTN_GROUP_WIDTH, LANES), F32)
    for i in range(steps):
        lses = [m + jnp.log(l) for m, l in stats[i]]
        top = jnp.maximum(jnp.maximum(lses[0], lses[1]), lses[2])
        es = [jnp.exp(x - top) for x in lses]
        den = es[0] + es[1] + es[2]
        col = sum(outs[i][g] * _expand_heads(es[g] / stats[i][g][1]) for g in range(N_DIL_GROUPS))
        cols = jnp.where(lane == i, col / _expand_heads(den), cols)
    return cols.T[0:steps, :]


def _sample_bias(rel_bias, steps):
    by_dist = [_group_bias(rel_bias, g).T for g in range(N_DIL_GROUPS)]
    win0 = DIL_PATTERNS[0][0]
    far_to_near = by_dist[0][:, ::-1]
    per_query = []
    for i in range(steps):
        cache = jnp.pad(far_to_near[:, :win0 - i], ((0, 0), (i, LANES - win0)), constant_values=NEG)
        new = jnp.pad(by_dist[0][:, :i + 1][:, ::-1], ((0, 0), (0, LANES - 1 - i)), constant_values=NEG)
        per_query.append(jnp.concatenate([cache, new], axis=1))
    tables = [jnp.stack(per_query).astype(F32)]
    for g in (1, 2):
        win, dil = DIL_PATTERNS[g]
        cache = jnp.repeat(by_dist[g][:, :0:-1], dil, axis=1)
        cache = jnp.where((np.arange(win) % dil < steps)[None], cache, NEG)
        new = jnp.pad(jnp.broadcast_to(by_dist[g][:, 0:1], (HEADS_PER_GROUP, steps)),
                      ((0, 0), (0, LANES - steps)), constant_values=NEG)
        tables.append(jnp.concatenate([cache, new], axis=1).astype(F32))
    return tables


def kernel(x_prompt, x_sample, state_pool, cache_kv_w128, cache_kv_w512, cache_kv_w2048, ln_g, ln_b, ffn_w13,
           ffn_w2, pool_w_in, pool_w_grp, pool_scale, pool_w_out, attn_w_kv, attn_w_q, attn_w_o, rel_bias):
    batch, seq, _ = x_prompt.shape
    dec_batch, steps, _ = x_sample.shape
    assert DEPTH == 2 and state_pool.shape[0] == 1 and attn_w_q.shape[0] == 1
    assert seq % ATTN_CHUNK == 0 and seq >= DIL_PATTERNS[-1][0]

    scale = pool_scale[0][None]
    ln = lambda layer, k: (ln_g[layer, k][None], ln_b[layer, k][None])
    q_scale = HEAD_DIM ** -0.5
    grp_rows = len(POOL_WINDOWS) * POOL_GROUP_DIM

    w13_00 = (ffn_w13, (0, 0))
    w2_00 = (ffn_w2, (0, 0))

    xp = x_prompt.reshape(batch * seq, D_MODEL)
    xs = jnp.swapaxes(x_sample, 0, 1).reshape(steps * dec_batch, D_MODEL)

    xp, xs, w13_01, w2_01, w_kv, w_in, w_grp, w_out = _ffn(
        xp, xs, w13_00, w2_00, *ln(0, 0),
        cast=((ffn_w13, (0, 1)), (ffn_w2, (0, 1)), (attn_w_kv, ()), (pool_w_in, (0,)),
              (pool_w_grp.reshape(1, grp_rows, POOL_GROUP_DIM), (0,)), (pool_w_out, (0,))))
    w_grp = w_grp.reshape(len(POOL_WINDOWS), POOL_GROUP_DIM, POOL_GROUP_DIM)
    xp, pool_p = _pool_prompt(xp, batch, w_in, w_grp, scale, w_out, *ln(0, 1))
    xs, pool_s = _pool_sample(xs, jnp.swapaxes(state_pool[0], 0, 1), steps, dec_batch,
                              w_in, w_grp, scale, w_out, *ln(0, 1))
    xp, xs, *kv_p, kv_s, w13_10, w2_10, w_q = _ffn(
        xp, xs, w13_01, w2_01, *ln(0, 2), wp=w_kv, proj_split=N_DIL_GROUPS,
        cast=((ffn_w13, (1, 0)), (ffn_w2, (1, 0)), (attn_w_q, (0,))))

    xp, xs, q_p, q_s, w13_11, w2_11, w_o = _ffn(
        xp, xs, w13_10, w2_10, *ln(1, 0), wp=w_q, proj_scales=(q_scale * LOG2E, q_scale),
        cast=((ffn_w13, (1, 1)), (ffn_w2, (1, 1)), (attn_w_o, (0,))))
    kv_p = [a.reshape(batch, seq, KV_WIDTH // N_DIL_GROUPS) for a in kv_p]
    attn_p, *kv_tails = _attn_prompt(q_p.reshape(batch, seq, Q_WIDTH), kv_p, _prompt_bias_rows(rel_bias))
    kv_s = jnp.swapaxes(kv_s.reshape(steps, dec_batch, KV_WIDTH), 0, 1)
    q_s = jnp.swapaxes(q_s.reshape(steps, dec_batch, Q_WIDTH), 0, 1)
    y_prompt, y_sample = _oproj_ffn(
        xp, attn_p.reshape(batch * seq, ATTN_GROUP_WIDTH), xs, q_s, kv_s,
        (cache_kv_w128, cache_kv_w512, cache_kv_w2048), _sample_bias(rel_bias, steps),
        w_o, *ln(1, 1), w13_11, w2_11, *ln(1, 2))

    y_prompt = y_prompt.reshape(batch, seq, D_MODEL)
    y_sample = jnp.swapaxes(y_sample.reshape(steps, dec_batch, D_MODEL), 0, 1)
    pool_p = pool_p[:, CARRY_ROWS - POOL_PAD:][None]
    pool_s = jnp.swapaxes(pool_s, 0, 1)[None]
    kv_p_groups = [jnp.transpose(t.reshape(batch, 2, HEADS_PER_GROUP, HEAD_DIM, w), (0, 4, 1, 2, 3))
                   for t, (w, _) in zip(kv_tails, DIL_PATTERNS)]
    kv_s = kv_s.reshape(dec_batch, steps, N_DIL_GROUPS, 2, HEADS_PER_GROUP, HEAD_DIM)

    return (y_prompt, y_sample, pool_p, pool_s, kv_p_groups[0], kv_p_groups[1], kv_p_groups[2],
            kv_s[:, :, 0], kv_s[:, :, 1], kv_s[:, :, 2])
```

```python
import functools

import numpy as np
import jax
import jax.numpy as jnp
from jax import lax
from jax.experimental import pallas as pl
from jax.experimental.pallas import tpu as pltpu

F32 = jnp.float32
BF16 = jnp.bfloat16

D_MODEL = 1024
DEPTH = 2
PAST_LEN = 16384
POOL_WINDOWS = (2, 4, 8, 16)
POOL_GROUP_DIM = D_MODEL // len(POOL_WINDOWS)
POOL_PAD = max(POOL_WINDOWS) - 1
DIL_PATTERNS = ((128, 1), (512, 4), (2048, 16))
N_DIL_GROUPS = len(DIL_PATTERNS)
HEAD_DIM = 64
HEADS_PER_GROUP = 8
ATTN_GROUP_WIDTH = HEADS_PER_GROUP * HEAD_DIM
Q_WIDTH = N_DIL_GROUPS * ATTN_GROUP_WIDTH
KV_WIDTH = 2 * Q_WIDTH
N_BUCKETS = 32
MAX_DISTANCE = 2048
D_FF = 2816
ALPHA = (2.0 * DEPTH) ** 0.25
LN_EPS = 1e-5
NEG = -1e30
LOG2E = 1.4426950408889634
N_KEYS = 129

LANES = 128
MXU_DIM = 256
VMEM_LIMIT_BYTES = 56 * 1024 * 1024
VMEM_LIMIT_FUSED_BYTES = 60 * 1024 * 1024

FF_CHUNK = MXU_DIM
ROW_TILE = 512
Q_BLOCK = 128
ATTN_CHUNK = 2048


def _bucket_table(dist):
    dist = np.asarray(dist, dtype=np.int64)
    max_exact = N_BUCKETS // 2
    ratio = np.maximum(dist, 1).astype(np.float64) / max_exact
    large = max_exact + (np.log(ratio) / np.log(MAX_DISTANCE / max_exact) * (N_BUCKETS - max_exact)).astype(np.int64)
    large = np.minimum(large, N_BUCKETS - 1)
    return np.where(dist < max_exact, dist, large).astype(np.int32)


def _layer_norm(z, g, b):
    mu = jnp.mean(z, axis=-1, keepdims=True)
    zc = z - mu
    var = jnp.mean(zc * zc, axis=-1, keepdims=True)
    return zc * lax.rsqrt(var + LN_EPS) * g + b


def _dot(a, b):
    return jnp.dot(a, b, preferred_element_type=F32)


def _swiglu(xb, w13_ref, w2_ref):
    acc = None
    for j in range(D_FF // FF_CHUNK):
        lo = j * FF_CHUNK
        gate = _dot(xb, w13_ref[:, lo:lo + FF_CHUNK].astype(BF16))
        up = _dot(xb, w13_ref[:, D_FF + lo:D_FF + lo + FF_CHUNK].astype(BF16))
        act = (gate * jax.nn.sigmoid(gate) * up).astype(BF16)
        y = _dot(act, w2_ref[lo:lo + FF_CHUNK, :].astype(BF16))
        acc = y if acc is None else acc + y
    return acc


def _ffn_kernel(*refs, proj_split, proj_scales, n_cast, prompt_steps):
    has_proj = proj_split > 0
    n_in = 6 + has_proj + n_cast
    x_refs = refs[:2]
    w13_ref, w2_ref, g_ref, b_ref = refs[2:6]
    o_refs = refs[n_in:n_in + 2]
    p_refs = refs[n_in + 2:n_in + 2 + proj_split]
    n_out = 2 + (proj_split + 1 if has_proj else 0)

    def run(x_ref, o_ref, p_refs, proj_scale):
        x = x_ref[...]
        y = _swiglu(x.astype(BF16), w13_ref, w2_ref)
        out = _layer_norm(ALPHA * x + 0.5 * y, g_ref[...], b_ref[...])
        o_ref[...] = out
        if has_proj:
            p = _dot(out.astype(BF16), refs[6][...]) * proj_scale
            width = p.shape[1] // len(p_refs)
            for j, p_ref in enumerate(p_refs):
                if len(p_ref.shape) == 2:
                    p_ref[...] = p[:, j * width:(j + 1) * width]
                else:
                    for s in range(p_ref.shape[0]):
                        p_ref[s] = p[:, j * width + s * LANES:j * width + (s + 1) * LANES]

    step = pl.program_id(0)
    pl.when(step < prompt_steps)(lambda: run(x_refs[0], o_refs[0], p_refs, proj_scales[0]))
    pl.when(step == prompt_steps)(
        lambda: run(x_refs[1], o_refs[1], refs[n_in + 2 + proj_split:n_in + n_out], proj_scales[1]))
    for src, dst in zip(refs[6 + has_proj:n_in], refs[n_in + n_out:]):
        dst[...] = src[...].astype(BF16)


def _resident(shape):
    nd = len(shape)
    return pl.BlockSpec(shape, lambda *_: (0,) * nd, pipeline_mode=pl.Buffered(1))


BF16_SUBLANES = 16


def _cast_chunks(n_rows, steps):
    chunks = steps
    while n_rows % chunks or (n_rows // chunks) % BF16_SUBLANES:
        chunks //= 2
    return chunks


def _whole(shape):
    nd = len(shape)
    return pl.BlockSpec(shape, lambda *_: (0,) * nd)


def _ffn(x, xs, w13, w2, g, b, wp=None, proj_scales=(1.0, 1.0), proj_split=1, cast=()):
    rows = x.shape[0]
    tm = ROW_TILE
    steps = rows // tm
    assert rows % tm == 0
    tile_index = lambda i: (jnp.minimum(i, steps - 1), 0)
    row_spec = pl.BlockSpec((tm, D_MODEL), tile_index)
    operands = [x, xs, w13, w2, g, b] + ([] if wp is None else [wp])
    in_specs = [row_spec] + [None if isinstance(a, tuple) else _resident(a.shape) for a in operands[1:]]
    for k in (2, 3):
        if isinstance(operands[k], tuple):
            operands[k], lead = operands[k]
            in_specs[k] = pl.BlockSpec((None,) * len(lead) + operands[k].shape[len(lead):],
                                       functools.partial(lambda i, lead: lead + (0, 0), lead=lead),
                                       pipeline_mode=pl.Buffered(1))
    out_specs = [row_spec, _whole(xs.shape)]
    out_shape = [jax.ShapeDtypeStruct((rows, D_MODEL), F32), jax.ShapeDtypeStruct(xs.shape, F32)]
    if wp is not None:
        assert wp.shape[1] % (proj_split * LANES) == 0
        width = wp.shape[1] // proj_split
        slabs = width // LANES
        out_specs += ([pl.BlockSpec((slabs, tm, LANES), lambda i: (0, jnp.minimum(i, steps - 1), 0))] * proj_split
                      + [_whole((xs.shape[0], wp.shape[1]))])
        out_shape += ([jax.ShapeDtypeStruct((slabs, rows, LANES), F32)] * proj_split
                      + [jax.ShapeDtypeStruct((xs.shape[0], wp.shape[1]), F32)])
    for w, lead in cast:
        n_rows, n_cols = w.shape[len(lead):]
        chunks = _cast_chunks(n_rows, steps)
        chunk_index = functools.partial(lambda i, last: (jnp.minimum(i, last), 0), last=chunks - 1)
        in_specs.append(pl.BlockSpec((None,) * len(lead) + (n_rows // chunks, n_cols),
                                     functools.partial(lambda i, lead, f: lead + f(i), lead=lead, f=chunk_index)))
        out_specs.append(pl.BlockSpec((n_rows // chunks, n_cols), chunk_index))
        out_shape.append(jax.ShapeDtypeStruct((n_rows, n_cols), BF16))
    return pl.pallas_call(
        functools.partial(_ffn_kernel, proj_split=0 if wp is None else proj_split, proj_scales=proj_scales,
                          n_cast=len(cast), prompt_steps=steps),
        grid=(steps + 1,), in_specs=in_specs, out_specs=out_specs, out_shape=out_shape,
        compiler_params=pltpu.CompilerParams(dimension_semantics=("arbitrary",),
                                             vmem_limit_bytes=VMEM_LIMIT_BYTES),
        name="ffn")(*operands, *[w for w, _ in cast])


def _oproj_ffn_kernel(x_ref, a_ref, xs_ref, q_ref, kv_ref, c0_ref, c1_ref, c2_ref, b0_ref, b1_ref, b2_ref,
                      wo_ref, g1_ref, be1_ref, w13_ref, w2_ref, g2_ref, be2_ref, o_ref, os_ref, attn_s,
                      *, prompt_steps, sample_steps, sample_batch):
    def run(x_ref, attn, o_ref):
        mix = _dot(attn.astype(BF16), wo_ref[...])
        x = _layer_norm(ALPHA * x_ref[...] + mix, g1_ref[...], be1_ref[...])
        y = _swiglu(x.astype(BF16), w13_ref, w2_ref)
        o_ref[...] = _layer_norm(ALPHA * x + 0.5 * y, g2_ref[...], be2_ref[...])

    step = pl.program_id(0)

    @pl.when(step < prompt_steps)
    def _():
        run(x_ref, jnp.concatenate([a_ref[j] for j in range(a_ref.shape[0])], axis=1), o_ref)
        rows = _attn_sample_rows(q_ref, kv_ref, (c0_ref, c1_ref, c2_ref), (b0_ref, b1_ref, b2_ref), sample_steps)
        for t in range(sample_steps):
            attn_s[pl.ds(t * sample_batch + step, 1), :] = rows[t:t + 1, :]

    @pl.when(step == prompt_steps)
    def _():
        run(xs_ref, attn_s[...], os_ref)


def _oproj_ffn(x, a, xs, q_s, kv_s, caches, bias_tables, wo, g1, b1, w13, w2, g2, b2):
    rows = x.shape[0]
    tm = ROW_TILE
    steps = rows // tm
    batch, sample_steps, _ = q_s.shape
    assert rows % tm == 0 and batch == steps and xs.shape[0] == batch * sample_steps
    tile = lambda i: jnp.minimum(i, steps - 1)
    row_spec = pl.BlockSpec((tm, D_MODEL), lambda i: (tile(i), 0))
    views, cache_specs = [], []
    for g, (win, dil) in enumerate(DIL_PATTERNS):
        assert caches[g].shape[1] == win and win % LANES == 0 and (g == 0 or sample_steps <= dil)
        views.append(jnp.transpose(caches[g], (0, 2, 3, 4, 1)).reshape(batch, 2, ATTN_GROUP_WIDTH, win))
        cache_specs.append(pl.BlockSpec((1, 2, ATTN_GROUP_WIDTH, win), lambda i: (tile(i), 0, 0, 0)))
    weights = (wo, g1, b1, w13, w2, g2, b2)
    return pl.pallas_call(
        functools.partial(_oproj_ffn_kernel, prompt_steps=steps, sample_steps=sample_steps, sample_batch=batch),
        grid=(steps + 1,),
        in_specs=[row_spec, pl.BlockSpec((a.shape[0], tm, LANES), lambda i: (0, tile(i), 0)), _resident(xs.shape),
                  pl.BlockSpec((1,) + q_s.shape[1:], lambda i: (tile(i), 0, 0)),
                  pl.BlockSpec((1,) + kv_s.shape[1:], lambda i: (tile(i), 0, 0))]
                 + cache_specs + [_resident(t.shape) for t in bias_tables] + [_resident(w.shape) for w in weights],
        out_specs=[row_spec, _whole(xs.shape)],
        out_shape=[jax.ShapeDtypeStruct((rows, D_MODEL), F32), jax.ShapeDtypeStruct(xs.shape, F32)],
        scratch_shapes=[pltpu.VMEM((xs.shape[0], ATTN_GROUP_WIDTH), F32)],
        compiler_params=pltpu.CompilerParams(dimension_semantics=("arbitrary",),
                                             vmem_limit_bytes=VMEM_LIMIT_FUSED_BYTES),
        name="oproj_ffn")(x, a, xs, q_s, kv_s, *views, *bias_tables, *weights)


CARRY_ROWS = 16
PAD_ROWS = 8
POOL_SUB_TILES = 2


def _pool_tail(x, u, pooled_groups, wgrp_ref, scale_ref, wout_ref, g_ref, b_ref):
    zs = [_dot(p.astype(BF16), wgrp_ref[gi]) for gi, p in enumerate(pooled_groups)]
    z = jnp.concatenate(zs, axis=-1) * scale_ref[...]
    mix = _dot(z.astype(BF16), wout_ref[...])
    return _layer_norm(ALPHA * x + mix, g_ref[...], b_ref[...])


def _pool_prompt_kernel(x_ref, win_ref, wgrp_ref, scale_ref, wout_ref, g_ref, b_ref, o_ref, st_ref,
                        ue_ref, s2_ref, s4_ref, s8_ref, *, tm, sub, tiles):
    t = pl.program_id(1)
    lo, base = PAD_ROWS, PAD_ROWS + CARRY_ROWS
    stages = (ue_ref, s2_ref, s4_ref, s8_ref)

    @pl.when(t == 0)
    def _():
        ue_ref[0:base, :] = jnp.zeros((base, D_MODEL), F32)
        for ref in stages[1:]:
            ref[0:lo, :] = jnp.zeros((lo, D_MODEL), F32)

    xs, us = [], []
    for first in range(0, tm, sub):
        xs.append(x_ref[first:first + sub, :])
        us.append(_dot(xs[-1].astype(BF16), win_ref[...]))
        ue_ref[base + first:base + first + sub, :] = us[-1]
    for j, first in enumerate(range(0, tm, sub)):
        r0, r1 = base + first, base + first + sub
        x, u = xs[j], us[j]
        pos = t * tm + first + lax.broadcasted_iota(jnp.int32, (sub, 1), 0)
        s0 = lo if first == 0 else r0
        pooled = []
        for gi, w in enumerate(POOL_WINDOWS):
            cols = slice(gi * POOL_GROUP_DIM, (gi + 1) * POOL_GROUP_DIM)
            n_stages = w.bit_length() - 1
            for level in range(n_stages - 1):
                src, dst, k = stages[level], stages[level + 1], 2 ** level
                dst[s0:r1, cols] = src[s0:r1, cols] + src[s0 - k:r1 - k, cols]
            src, k = stages[n_stages - 1], w // 2
            win_sum = src[r0:r1, cols] + src[r0 - k:r1 - k, cols]
            inv_count = 1.0 / jnp.minimum(pos + 1, w).astype(F32)
            pooled.append(win_sum * inv_count - u[:, cols])
        o_ref[first:first + sub, :] = _pool_tail(x, u, pooled, wgrp_ref, scale_ref, wout_ref, g_ref, b_ref)

    tail = ue_ref[base + tm - CARRY_ROWS:base + tm, :]

    @pl.when(t == tiles - 1)
    def _():
        st_ref[0] = tail

    ue_ref[lo:base, :] = tail


def _pool_prompt(x, batch, w_in, w_grp, scale, w_out, g, b):
    rows = x.shape[0]
    seq = rows // batch
    sub = ROW_TILE
    tm = POOL_SUB_TILES * sub
    assert seq % tm == 0
    tiles = seq // tm
    row_spec = pl.BlockSpec((tm, D_MODEL), lambda bi, t: (bi * tiles + t, 0))
    return pl.pallas_call(
        functools.partial(_pool_prompt_kernel, tm=tm, sub=sub, tiles=tiles), grid=(batch, tiles),
        in_specs=[row_spec, _resident(w_in.shape), _resident(w_grp.shape), _resident(scale.shape),
                  _resident(w_out.shape), _resident(g.shape), _resident(b.shape)],
        out_specs=[row_spec, pl.BlockSpec((1, CARRY_ROWS, D_MODEL), lambda bi, t: (bi, 0, 0))],
        out_shape=[jax.ShapeDtypeStruct((rows, D_MODEL), F32),
                   jax.ShapeDtypeStruct((batch, CARRY_ROWS, D_MODEL), F32)],
        scratch_shapes=[pltpu.VMEM((PAD_ROWS + CARRY_ROWS + tm, D_MODEL), F32)] * 4,
        compiler_params=pltpu.CompilerParams(dimension_semantics=("arbitrary", "arbitrary"),
                                             vmem_limit_bytes=VMEM_LIMIT_BYTES),
        name="pool_prompt")(x, w_in, w_grp, scale, w_out, g, b)


def _pool_sample_kernel(x_ref, st_ref, win_ref, wgrp_ref, scale_ref, wout_ref, g_ref, b_ref, o_ref, nst_ref,
                        *, steps, batch):
    x = x_ref[...]
    u = _dot(x.astype(BF16), win_ref[...])
    ext = [st_ref[k] for k in range(POOL_PAD)] + [u[i * batch:(i + 1) * batch, :] for i in range(steps)]
    pooled = []
    for gi, w in enumerate(POOL_WINDOWS):
        cols = slice(gi * POOL_GROUP_DIM, (gi + 1) * POOL_GROUP_DIM)
        per_step = []
        for i in range(steps):
            win_sum = ext[POOL_PAD + i][:, cols]
            for k in range(1, w):
                win_sum = win_sum + ext[POOL_PAD + i - k][:, cols]
            count = float(min(PAST_LEN + i + 1, w))
            per_step.append(win_sum / count - ext[POOL_PAD + i][:, cols])
        pooled.append(jnp.concatenate(per_step, axis=0))
    o_ref[...] = _pool_tail(x, u, pooled, wgrp_ref, scale_ref, wout_ref, g_ref, b_ref)
    for k in range(POOL_PAD):
        nst_ref[k] = ext[steps + k]


def _pool_sample(x, st, steps, batch, w_in, w_grp, scale, w_out, g, b):
    rows = x.shape[0]
    full = lambda shape: pl.BlockSpec(shape, lambda i: (0,) * len(shape))
    return pl.pallas_call(
        functools.partial(_pool_sample_kernel, steps=steps, batch=batch), grid=(1,),
        in_specs=[full(x.shape), full(st.shape), full(w_in.shape), full(w_grp.shape), full(scale.shape),
                  full(w_out.shape), full(g.shape), full(b.shape)],
        out_specs=[full(x.shape), full(st.shape)],
        out_shape=[jax.ShapeDtypeStruct((rows, D_MODEL), F32), jax.ShapeDtypeStruct(st.shape, F32)],
        compiler_params=pltpu.CompilerParams(dimension_semantics=("arbitrary",),
                                             vmem_limit_bytes=VMEM_LIMIT_BYTES),
        name="pool_sample")(x, st, w_in, w_grp, scale, w_out, g, b)


def _attn_prompt_kernel(*refs):
    n = N_DIL_GROUPS
    q_refs, k_refs, v_refs, kp_refs, vp_refs = (refs[i * n:(i + 1) * n] for i in range(5))
    bias_ref, o_ref = refs[5 * n:5 * n + 2]
    tail_refs = refs[5 * n + 2:6 * n + 2]
    oslab, lslab, btile = refs[6 * n + 2:]
    c = pl.program_id(2)
    key_col = lax.broadcasted_iota(jnp.int32, (1, 2 * Q_BLOCK), 1)

    @pl.when(c == 0)
    def _():
        for g in range(N_DIL_GROUPS):
            for hh in range(2):
                row = jnp.broadcast_to(bias_ref[g, 0, hh:hh + 1, :] * LOG2E, (Q_BLOCK, 2 * Q_BLOCK))
                tile = pltpu.roll(row, 0, 1, stride=1, stride_axis=0)
                btile[g, hh, 0] = tile
                btile[g, hh, 1] = jnp.where(key_col < Q_BLOCK, NEG, tile)

    lane = lax.broadcasted_iota(jnp.int32, (1, LANES), 1)
    first_head = lane < HEAD_DIM
    ones = jnp.ones((2 * Q_BLOCK, LANES), BF16)
    assert DIL_PATTERNS[0][1] == 1

    def attend(first_chunk):
        for g in reversed(range(N_DIL_GROUPS)):
            dil = DIL_PATTERNS[g][1]
            sub_blocks = ATTN_CHUNK // (Q_BLOCK * dil)
            span = Q_BLOCK * dil

            def rows(res, sb, dil=dil, span=span):
                start = res + sb * span
                return pl.ds(start, Q_BLOCK) if dil == 1 else pl.ds(start, Q_BLOCK, stride=dil)

            for res in range(dil):
                k_prev = kp_refs[g][0, rows(res, 0), :].astype(BF16)
                v_prev = vp_refs[g][0, rows(res, 0), :].astype(BF16)
                for sb in range(sub_blocks):
                    k_cur = k_refs[g][0, rows(res, sb), :].astype(BF16)
                    v_cur = v_refs[g][0, rows(res, sb), :].astype(BF16)
                    kk = jnp.concatenate([k_prev, k_cur], axis=0)
                    vv = jnp.concatenate([jnp.concatenate([v_prev, v_cur], axis=0), ones], axis=1)
                    qb = q_refs[g][0, rows(res, sb), :].astype(BF16)
                    no_prev = int(first_chunk and sb == 0)
                    outs, maxes, sums = [], [], []
                    for hh in range(2):
                        head_lanes = first_head if hh == 0 else jnp.logical_not(first_head)
                        qh = jnp.where(head_lanes, qb, jnp.zeros_like(qb))
                        s = lax.dot_general(qh, kk, (((1,), (1,)), ((), ())), preferred_element_type=F32)
                        s = s + btile[g, hh, no_prev]
                        m = jnp.max(s, axis=1, keepdims=True)
                        pv = _dot(jnp.exp2(s - m).astype(BF16), vv)
                        outs.append(pv[:, :LANES])
                        maxes.append(m)
                        sums.append(pv[:, LANES:])
                    row_sum = jnp.where(first_head, sums[0], sums[1])
                    out = jnp.where(first_head, outs[0], outs[1]) * (1.0 / row_sum)
                    lse = jnp.where(first_head, maxes[0], maxes[1]) + jnp.log2(row_sum)
                    k_prev, v_prev = k_cur, v_cur
                    if g > 0:
                        oslab[g - 1, rows(res, sb), :] = out
                        lslab[g - 1, rows(res, sb), :] = lse
                        continue
                    blk = rows(res, sb)
                    lses = [lse] + [lslab[j, blk, :] for j in range(N_DIL_GROUPS - 1)]
                    top = functools.reduce(jnp.maximum, lses)
                    es = [jnp.exp2(x - top) for x in lses]
                    den, num = es[0], es[0] * out
                    for j in range(N_DIL_GROUPS - 1):
                        den = den + es[j + 1]
                        num = num + es[j + 1] * oslab[j, blk, :]
                    o_ref[0, blk, :] = (num / den).astype(o_ref.dtype)

    pl.when(c == 0)(lambda: attend(True))
    pl.when(c != 0)(lambda: attend(False))

    @pl.when(c == pl.num_programs(2) - 1)
    def _():
        for g, (win, _) in enumerate(DIL_PATTERNS):
            tail_refs[g][0, 0] = k_refs[g][0, ATTN_CHUNK - win:, :].T
            tail_refs[g][0, 1] = v_refs[g][0, ATTN_CHUNK - win:, :].T


def _group_bias(rel_bias, g):
    buckets = _bucket_table(np.arange(N_KEYS) * DIL_PATTERNS[g][1])
    return rel_bias[buckets][:, g * HEADS_PER_GROUP:(g + 1) * HEADS_PER_GROUP]


def _prompt_bias_rows(rel_bias):
    rows = []
    for g in range(N_DIL_GROUPS):
        by_key = _group_bias(rel_bias, g)[::-1].T
        rows.append(jnp.pad(by_key, ((0, 0), (0, 2 * Q_BLOCK - N_KEYS)), constant_values=NEG))
    return jnp.stack(rows).reshape(N_DIL_GROUPS, HEADS_PER_GROUP // 2, 2, 2 * Q_BLOCK).astype(F32)


def _attn_prompt(q, kv, batch, bias_rows):
    pairs = ATTN_GROUP_WIDTH // LANES
    seq = q.shape[1] // batch
    chunks = seq // ATTN_CHUNK
    blk = (1, ATTN_CHUNK, LANES)

    def spec(lane_block):
        return pl.BlockSpec(blk, lambda b, hp, c: (lane_block * pairs + hp, b * chunks + c, 0))

    def prev_spec(lane_block, dil):
        spans = ATTN_CHUNK // (Q_BLOCK * dil)
        return pl.BlockSpec((1, Q_BLOCK * dil, LANES),
                            lambda b, hp, c: (lane_block * pairs + hp,
                                              b * chunks * spans + jnp.maximum(c * spans - 1, 0), 0))

    groups = range(N_DIL_GROUPS)
    dils = [dil for _, dil in DIL_PATTERNS]
    in_specs = ([spec(g) for g in groups] + [spec(0)] * N_DIL_GROUPS + [spec(1)] * N_DIL_GROUPS
                + [prev_spec(0, dils[g]) for g in groups] + [prev_spec(1, dils[g]) for g in groups]
                + [pl.BlockSpec((N_DIL_GROUPS, 1, 2, 2 * Q_BLOCK), lambda b, hp, c: (0, hp, 0, 0))])
    return pl.pallas_call(
        _attn_prompt_kernel, grid=(batch, pairs, chunks), in_specs=in_specs,
        out_specs=[pl.BlockSpec(blk, lambda b, hp, c: (hp, b * chunks + c, 0))]
                  + [pl.BlockSpec((1, 2, LANES, win), lambda b, hp, c: (b, 0, hp, 0)) for win, _ in DIL_PATTERNS],
        out_shape=[jax.ShapeDtypeStruct((pairs, batch * seq, LANES), BF16)]
                  + [jax.ShapeDtypeStruct((batch, 2, ATTN_GROUP_WIDTH, win), F32) for win, _ in DIL_PATTERNS],
        scratch_shapes=[pltpu.VMEM((N_DIL_GROUPS - 1, ATTN_CHUNK, LANES), F32)] * 2
                       + [pltpu.VMEM((N_DIL_GROUPS, 2, 2, Q_BLOCK, 2 * Q_BLOCK), F32)],
        compiler_params=pltpu.CompilerParams(dimension_semantics=("arbitrary", "arbitrary", "arbitrary"),
                                             vmem_limit_bytes=VMEM_LIMIT_BYTES),
        name="attn_prompt")(*([q] * N_DIL_GROUPS + list(kv) * 4), bias_rows)


def _expand_heads(x):
    width = x.shape[-1]
    return jnp.broadcast_to(x[:, None, :], (HEADS_PER_GROUP, HEAD_DIM, width)).reshape(ATTN_GROUP_WIDTH, width)


def _head_sums(x):
    return x.reshape(HEADS_PER_GROUP, HEAD_DIM, x.shape[-1]).sum(axis=1)


def _softmax_parts(s):
    m = jnp.max(s, axis=1, keepdims=True)
    p = jnp.exp(s - m)
    return m, p, jnp.sum(p, axis=1, keepdims=True)


def _attn_sample_rows(q_ref, kv_ref, cache_refs, bias_refs, steps):
    c0_ref, c1_ref, c2_ref = cache_refs
    b0_ref, b1_ref, b2_ref = bias_refs
    lane = lax.broadcasted_iota(jnp.int32, (1, LANES), 1)
    sub = lax.broadcasted_iota(jnp.int32, (LANES, 1), 0)
    group_cols = KV_WIDTH // N_DIL_GROUPS

    def lanes_from_rows(mat, period):
        res = sub & (period - 1)
        rows = jnp.zeros((LANES, ATTN_GROUP_WIDTH), F32)
        for i in range(steps):
            rows = jnp.where(res == i, mat[i:i + 1, :], rows)
        return rows.T

    def new_rows(g, which, period):
        lo = g * group_cols + which * ATTN_GROUP_WIDTH
        return lanes_from_rows(kv_ref[0, :, lo:lo + ATTN_GROUP_WIDTH], period)

    outs = [[None] * N_DIL_GROUPS for _ in range(steps)]
    stats = [[None] * N_DIL_GROUPS for _ in range(steps)]

    q0 = q_ref[0, :, 0:ATTN_GROUP_WIDTH]
    kn, vn = new_rows(0, 0, LANES), new_rows(0, 1, LANES)
    for i in range(steps):
        qpat = jnp.broadcast_to(q0[i:i + 1, :], (LANES, ATTN_GROUP_WIDTH)).T
        s = jnp.concatenate([_head_sums(c0_ref[0, 0] * qpat), _head_sums(kn * qpat)], axis=1) + b0_ref[i]
        m, p, l = _softmax_parts(s)
        acc = c0_ref[0, 1] * _expand_heads(p[:, :LANES]) + vn * _expand_heads(p[:, LANES:])
        outs[i][0] = jnp.sum(acc, axis=1, keepdims=True)
        stats[i][0] = (m, l)

    for g, c_ref, b_ref in ((1, c1_ref, b1_ref), (2, c2_ref, b2_ref)):
        win, dil = DIL_PATTERNS[g]
        tiles = win // LANES
        qpat = lanes_from_rows(q_ref[0, :, g * ATTN_GROUP_WIDTH:(g + 1) * ATTN_GROUP_WIDTH], dil)
        kn, vn = new_rows(g, 0, dil), new_rows(g, 1, dil)
        parts = [_head_sums(c_ref[0, 0, :, t * LANES:(t + 1) * LANES] * qpat) for t in range(tiles)]
        s_all = jnp.concatenate(parts + [_head_sums(kn * qpat)], axis=1) + b_ref[...]
        res_all = lax.broadcasted_iota(jnp.int32, (1, win + LANES), 1) & (dil - 1)
        p_all = jnp.zeros_like(s_all)
        for i in range(steps):
            m, p, l = _softmax_parts(jnp.where(res_all == i, s_all, NEG))
            p_all = p_all + p
            stats[i][g] = (m, l)
        acc = vn * _expand_heads(p_all[:, win:])
        for t in range(tiles):
            acc = acc + c_ref[0, 1, :, t * LANES:(t + 1) * LANES] * _expand_heads(p_all[:, t * LANES:(t + 1) * LANES])
        res = lane & (dil - 1)
        for i in range(steps):
            outs[i][g] = jnp.sum(jnp.where(res == i, acc, 0.0), axis=1, keepdims=True)

    cols = jnp.zeros((ATTN_GROUP_WIDTH, LANES), F32)
    for i in range(steps):
        lses = [m + jnp.log(l) for m, l in stats[i]]
        top = jnp.maximum(jnp.maximum(lses[0], lses[1]), lses[2])
        es = [jnp.exp(x - top) for x in lses]
        den = es[0] + es[1] + es[2]
        col = sum(outs[i][g] * _expand_heads(es[g] / stats[i][g][1]) for g in range(N_DIL_GROUPS))
        cols = jnp.where(lane == i, col / _expand_heads(den), cols)
    return cols.T[0:steps, :]


def _sample_bias(rel_bias, steps):
    by_dist = [_group_bias(rel_bias, g).T for g in range(N_DIL_GROUPS)]
    win0 = DIL_PATTERNS[0][0]
    far_to_near = by_dist[0][:, ::-1]
    per_query = []
    for i in range(steps):
        cache = jnp.pad(far_to_near[:, :win0 - i], ((0, 0), (i, LANES - win0)), constant_values=NEG)
        new = jnp.pad(by_dist[0][:, :i + 1][:, ::-1], ((0, 0), (0, LANES - 1 - i)), constant_values=NEG)
        per_query.append(jnp.concatenate([cache, new], axis=1))
    tables = [jnp.stack(per_query).astype(F32)]
    for g in (1, 2):
        win, dil = DIL_PATTERNS[g]
        cache = jnp.repeat(by_dist[g][:, :0:-1], dil, axis=1)
        cache = jnp.where((np.arange(win) % dil < steps)[None], cache, NEG)
        new = jnp.pad(jnp.broadcast_to(by_dist[g][:, 0:1], (HEADS_PER_GROUP, steps)),
                      ((0, 0), (0, LANES - steps)), constant_values=NEG)
        tables.append(jnp.concatenate([cache, new], axis=1).astype(F32))
    return tables


def kernel(x_prompt, x_sample, state_pool, cache_kv_w128, cache_kv_w512, cache_kv_w2048, ln_g, ln_b, ffn_w13,
           ffn_w2, pool_w_in, pool_w_grp, pool_scale, pool_w_out, attn_w_kv, attn_w_q, attn_w_o, rel_bias):
    batch, seq, _ = x_prompt.shape
    dec_batch, steps, _ = x_sample.shape
    assert DEPTH == 2 and state_pool.shape[0] == 1 and attn_w_q.shape[0] == 1
    assert seq % ATTN_CHUNK == 0 and seq >= DIL_PATTERNS[-1][0]

    scale = pool_scale[0][None]
    ln = lambda layer, k: (ln_g[layer, k][None], ln_b[layer, k][None])
    q_scale = HEAD_DIM ** -0.5
    grp_rows = len(POOL_WINDOWS) * POOL_GROUP_DIM

    w13_00 = (ffn_w13, (0, 0))
    w2_00 = (ffn_w2, (0, 0))

    xp = x_prompt.reshape(batch * seq, D_MODEL)
    xs = jnp.swapaxes(x_sample, 0, 1).reshape(steps * dec_batch, D_MODEL)

    xp, xs, w13_01, w2_01, w_kv, w_in, w_grp, w_out = _ffn(
        xp, xs, w13_00, w2_00, *ln(0, 0),
        cast=((ffn_w13, (0, 1)), (ffn_w2, (0, 1)), (attn_w_kv, ()), (pool_w_in, (0,)),
              (pool_w_grp.reshape(1, grp_rows, POOL_GROUP_DIM), (0,)), (pool_w_out, (0,))))
    w_grp = w_grp.reshape(len(POOL_WINDOWS), POOL_GROUP_DIM, POOL_GROUP_DIM)
    xp, pool_p = _pool_prompt(xp, batch, w_in, w_grp, scale, w_out, *ln(0, 1))
    xs, pool_s = _pool_sample(xs, jnp.swapaxes(state_pool[0], 0, 1), steps, dec_batch,
                              w_in, w_grp, scale, w_out, *ln(0, 1))
    xp, xs, *kv_p, kv_s, w13_10, w2_10, w_q = _ffn(
        xp, xs, w13_01, w2_01, *ln(0, 2), wp=w_kv, proj_split=N_DIL_GROUPS,
        cast=((ffn_w13, (1, 0)), (ffn_w2, (1, 0)), (attn_w_q, (0,))))

    xp, xs, q_p, q_s, w13_11, w2_11, w_o = _ffn(
        xp, xs, w13_10, w2_10, *ln(1, 0), wp=w_q, proj_scales=(q_scale * LOG2E, q_scale),
        cast=((ffn_w13, (1, 1)), (ffn_w2, (1, 1)), (attn_w_o, (0,))))
    attn_p, *kv_tails = _attn_prompt(q_p, kv_p, batch, _prompt_bias_rows(rel_bias))
    kv_s = jnp.swapaxes(kv_s.reshape(steps, dec_batch, KV_WIDTH), 0, 1)
    q_s = jnp.swapaxes(q_s.reshape(steps, dec_batch, Q_WIDTH), 0, 1)
    y_prompt, y_sample = _oproj_ffn(
        xp, attn_p, xs, q_s, kv_s,
        (cache_kv_w128, cache_kv_w512, cache_kv_w2048), _sample_bias(rel_bias, steps),
        w_o, *ln(1, 1), w13_11, w2_11, *ln(1, 2))

    y_prompt = y_prompt.reshape(batch, seq, D_MODEL)
    y_sample = jnp.swapaxes(y_sample.reshape(steps, dec_batch, D_MODEL), 0, 1)
    pool_p = pool_p[:, CARRY_ROWS - POOL_PAD:][None]
    pool_s = jnp.swapaxes(pool_s, 0, 1)[None]
    kv_p_groups = [jnp.transpose(t.reshape(batch, 2, HEADS_PER_GROUP, HEAD_DIM, w), (0, 4, 1, 2, 3))
                   for t, (w, _) in zip(kv_tails, DIL_PATTERNS)]
    kv_s = kv_s.reshape(dec_batch, steps, N_DIL_GROUPS, 2, HEADS_PER_GROUP, HEAD_DIM)

    return (y_prompt, y_sample, pool_p, pool_s, kv_p_groups[0], kv_p_groups[1], kv_p_groups[2],
            kv_s[:, :, 0], kv_s[:, :, 1], kv_s[:, :, 2])
```

```python
import functools

import numpy as np
import jax
import jax.numpy as jnp
from jax import lax
from jax.experimental import pallas as pl
from jax.experimental.pallas import tpu as pltpu

F32 = jnp.float32
BF16 = jnp.bfloat16

D_MODEL = 1024
DEPTH = 2
PAST_LEN = 16384
POOL_WINDOWS = (2, 4, 8, 16)
POOL_GROUP_DIM = D_MODEL // len(POOL_WINDOWS)
POOL_PAD = max(POOL_WINDOWS) - 1
DIL_PATTERNS = ((128, 1), (512, 4), (2048, 16))
N_DIL_GROUPS = len(DIL_PATTERNS)
HEAD_DIM = 64
HEADS_PER_GROUP = 8
ATTN_GROUP_WIDTH = HEADS_PER_GROUP * HEAD_DIM
Q_WIDTH = N_DIL_GROUPS * ATTN_GROUP_WIDTH
KV_WIDTH = 2 * Q_WIDTH
N_BUCKETS = 32
MAX_DISTANCE = 2048
D_FF = 2816
ALPHA = (2.0 * DEPTH) ** 0.25
LN_EPS = 1e-5
NEG = -1e30
LOG2E = 1.4426950408889634
N_KEYS = 129

LANES = 128
MXU_DIM = 256
VMEM_LIMIT_BYTES = 56 * 1024 * 1024
VMEM_LIMIT_FUSED_BYTES = 60 * 1024 * 1024

FF_CHUNK = MXU_DIM
ROW_TILE = 512
Q_BLOCK = 128
ATTN_CHUNK = 2048


def _bucket_table(dist):
    dist = np.asarray(dist, dtype=np.int64)
    max_exact = N_BUCKETS // 2
    ratio = np.maximum(dist, 1).astype(np.float64) / max_exact
    large = max_exact + (np.log(ratio) / np.log(MAX_DISTANCE / max_exact) * (N_BUCKETS - max_exact)).astype(np.int64)
    large = np.minimum(large, N_BUCKETS - 1)
    return np.where(dist < max_exact, dist, large).astype(np.int32)


def _layer_norm(z, g, b):
    mu = jnp.mean(z, axis=-1, keepdims=True)
    zc = z - mu
    var = jnp.mean(zc * zc, axis=-1, keepdims=True)
    return zc * lax.rsqrt(var + LN_EPS) * g + b


def _dot(a, b):
    return jnp.dot(a, b, preferred_element_type=F32)


def _swiglu(xb, w13_ref, w2_ref):
    acts = []
    for j in range(D_FF // FF_CHUNK):
        lo = j * FF_CHUNK
        gate = _dot(xb, w13_ref[:, lo:lo + FF_CHUNK].astype(BF16))
        up = _dot(xb, w13_ref[:, D_FF + lo:D_FF + lo + FF_CHUNK].astype(BF16))
        acts.append((gate * jax.nn.sigmoid(gate) * up).astype(BF16))
    return _dot(jnp.concatenate(acts, axis=1), w2_ref[...].astype(BF16))


def _ffn_kernel(*refs, proj_split, proj_scales, n_cast, prompt_steps):
    has_proj = proj_split > 0
    n_in = 6 + has_proj + n_cast
    x_refs = refs[:2]
    w13_ref, w2_ref, g_ref, b_ref = refs[2:6]
    o_refs = refs[n_in:n_in + 2]
    p_refs = refs[n_in + 2:n_in + 2 + proj_split]
    n_out = 2 + (proj_split + 1 if has_proj else 0)

    def run(x_ref, o_ref, p_refs, proj_scale):
        x = x_ref[...]
        y = _swiglu(x.astype(BF16), w13_ref, w2_ref)
        out = _layer_norm(ALPHA * x + 0.5 * y, g_ref[...], b_ref[...])
        o_ref[...] = out
        if has_proj:
            p = _dot(out.astype(BF16), refs[6][...]) * proj_scale
            width = p.shape[1] // len(p_refs)
            for j, p_ref in enumerate(p_refs):
                if len(p_ref.shape) == 2:
                    p_ref[...] = p[:, j * width:(j + 1) * width]
                else:
                    for s in range(p_ref.shape[0]):
                        p_ref[s] = p[:, j * width + s * LANES:j * width + (s + 1) * LANES]

    step = pl.program_id(0)
    pl.when(step < prompt_steps)(lambda: run(x_refs[0], o_refs[0], p_refs, proj_scales[0]))
    pl.when(step == prompt_steps)(
        lambda: run(x_refs[1], o_refs[1], refs[n_in + 2 + proj_split:n_in + n_out], proj_scales[1]))
    for src, dst in zip(refs[6 + has_proj:n_in], refs[n_in + n_out:]):
        dst[...] = src[...].astype(BF16)


def _resident(shape):
    nd = len(shape)
    return pl.BlockSpec(shape, lambda *_: (0,) * nd, pipeline_mode=pl.Buffered(1))


BF16_SUBLANES = 16


def _cast_chunks(n_rows, steps):
    chunks = steps
    while n_rows % chunks or (n_rows // chunks) % BF16_SUBLANES:
        chunks //= 2
    return chunks


def _whole(shape):
    nd = len(shape)
    return pl.BlockSpec(shape, lambda *_: (0,) * nd)


def _ffn(x, xs, w13, w2, g, b, wp=None, proj_scales=(1.0, 1.0), proj_split=1, cast=()):
    rows = x.shape[0]
    tm = ROW_TILE
    steps = rows // tm
    assert rows % tm == 0
    tile_index = lambda i: (jnp.minimum(i, steps - 1), 0)
    row_spec = pl.BlockSpec((tm, D_MODEL), tile_index)
    operands = [x, xs, w13, w2, g, b] + ([] if wp is None else [wp])
    in_specs = [row_spec] + [None if isinstance(a, tuple) else _resident(a.shape) for a in operands[1:]]
    for k in (2, 3):
        if isinstance(operands[k], tuple):
            operands[k], lead = operands[k]
            in_specs[k] = pl.BlockSpec((None,) * len(lead) + operands[k].shape[len(lead):],
                                       functools.partial(lambda i, lead: lead + (0, 0), lead=lead),
                                       pipeline_mode=pl.Buffered(1))
    out_specs = [row_spec, _whole(xs.shape)]
    out_shape = [jax.ShapeDtypeStruct((rows, D_MODEL), F32), jax.ShapeDtypeStruct(xs.shape, F32)]
    if wp is not None:
        assert wp.shape[1] % (proj_split * LANES) == 0
        width = wp.shape[1] // proj_split
        slabs = width // LANES
        out_specs += ([pl.BlockSpec((slabs, tm, LANES), lambda i: (0, jnp.minimum(i, steps - 1), 0))] * proj_split
                      + [_whole((xs.shape[0], wp.shape[1]))])
        out_shape += ([jax.ShapeDtypeStruct((slabs, rows, LANES), F32)] * proj_split
                      + [jax.ShapeDtypeStruct((xs.shape[0], wp.shape[1]), F32)])
    for w, lead in cast:
        n_rows, n_cols = w.shape[len(lead):]
        chunks = _cast_chunks(n_rows, steps)
        chunk_index = functools.partial(lambda i, last: (jnp.minimum(i, last), 0), last=chunks - 1)
        in_specs.append(pl.BlockSpec((None,) * len(lead) + (n_rows // chunks, n_cols),
                                     functools.partial(lambda i, lead, f: lead + f(i), lead=lead, f=chunk_index)))
        out_specs.append(pl.BlockSpec((n_rows // chunks, n_cols), chunk_index))
        out_shape.append(jax.ShapeDtypeStruct((n_rows, n_cols), BF16))
    return pl.pallas_call(
        functools.partial(_ffn_kernel, proj_split=0 if wp is None else proj_split, proj_scales=proj_scales,
                          n_cast=len(cast), prompt_steps=steps),
        grid=(steps + 1,), in_specs=in_specs, out_specs=out_specs, out_shape=out_shape,
        compiler_params=pltpu.CompilerParams(dimension_semantics=("arbitrary",),
                                             vmem_limit_bytes=VMEM_LIMIT_BYTES),
        name="ffn")(*operands, *[w for w, _ in cast])


def _oproj_ffn_kernel(x_ref, a_ref, xs_ref, q_ref, kv_ref, c0_ref, c1_ref, c2_ref, b0_ref, b1_ref, b2_ref,
                      wo_ref, g1_ref, be1_ref, w13_ref, w2_ref, g2_ref, be2_ref, o_ref, os_ref, attn_s,
                      *, prompt_steps, sample_steps, sample_batch):
    def run(x_ref, attn, o_ref):
        mix = _dot(attn.astype(BF16), wo_ref[...])
        x = _layer_norm(ALPHA * x_ref[...] + mix, g1_ref[...], be1_ref[...])
        y = _swiglu(x.astype(BF16), w13_ref, w2_ref)
        o_ref[...] = _layer_norm(ALPHA * x + 0.5 * y, g2_ref[...], be2_ref[...])

    step = pl.program_id(0)

    @pl.when(step < prompt_steps)
    def _():
        run(x_ref, jnp.concatenate([a_ref[j] for j in range(a_ref.shape[0])], axis=1), o_ref)
        rows = _attn_sample_rows(q_ref, kv_ref, (c0_ref, c1_ref, c2_ref), (b0_ref, b1_ref, b2_ref), sample_steps)
        for t in range(sample_steps):
            attn_s[pl.ds(t * sample_batch + step, 1), :] = rows[t:t + 1, :]

    @pl.when(step == prompt_steps)
    def _():
        run(xs_ref, attn_s[...], os_ref)


def _oproj_ffn(x, a, xs, q_s, kv_s, caches, bias_tables, wo, g1, b1, w13, w2, g2, b2):
    rows = x.shape[0]
    tm = ROW_TILE
    steps = rows // tm
    batch, sample_steps, _ = q_s.shape
    assert rows % tm == 0 and batch == steps and xs.shape[0] == batch * sample_steps
    tile = lambda i: jnp.minimum(i, steps - 1)
    row_spec = pl.BlockSpec((tm, D_MODEL), lambda i: (tile(i), 0))
    views, cache_specs = [], []
    for g, (win, dil) in enumerate(DIL_PATTERNS):
        assert caches[g].shape[1] == win and win % LANES == 0 and (g == 0 or sample_steps <= dil)
        views.append(jnp.transpose(caches[g], (0, 2, 3, 4, 1)).reshape(batch, 2, ATTN_GROUP_WIDTH, win))
        cache_specs.append(pl.BlockSpec((1, 2, ATTN_GROUP_WIDTH, win), lambda i: (tile(i), 0, 0, 0)))
    weights = (wo, g1, b1, w13, w2, g2, b2)
    return pl.pallas_call(
        functools.partial(_oproj_ffn_kernel, prompt_steps=steps, sample_steps=sample_steps, sample_batch=batch),
        grid=(steps + 1,),
        in_specs=[row_spec, pl.BlockSpec((a.shape[0], tm, LANES), lambda i: (0, tile(i), 0)), _resident(xs.shape),
                  pl.BlockSpec((1,) + q_s.shape[1:], lambda i: (tile(i), 0, 0)),
                  pl.BlockSpec((1,) + kv_s.shape[1:], lambda i: (tile(i), 0, 0))]
                 + cache_specs + [_resident(t.shape) for t in bias_tables] + [_resident(w.shape) for w in weights],
        out_specs=[row_spec, _whole(xs.shape)],
        out_shape=[jax.ShapeDtypeStruct((rows, D_MODEL), F32), jax.ShapeDtypeStruct(xs.shape, F32)],
        scratch_shapes=[pltpu.VMEM((xs.shape[0], ATTN_GROUP_WIDTH), F32)],
        compiler_params=pltpu.CompilerParams(dimension_semantics=("arbitrary",),
                                             vmem_limit_bytes=VMEM_LIMIT_FUSED_BYTES),
        name="oproj_ffn")(x, a, xs, q_s, kv_s, *views, *bias_tables, *weights)


CARRY_ROWS = 16
PAD_ROWS = 8
POOL_SUB_TILES = 2


def _pool_tail(x, u, pooled_groups, wgrp_ref, scale_ref, wout_ref, g_ref, b_ref):
    zs = [_dot(p.astype(BF16), wgrp_ref[gi]) for gi, p in enumerate(pooled_groups)]
    z = jnp.concatenate(zs, axis=-1) * scale_ref[...]
    mix = _dot(z.astype(BF16), wout_ref[...])
    return _layer_norm(ALPHA * x + mix, g_ref[...], b_ref[...])


def _pool_prompt_kernel(x_ref, win_ref, wgrp_ref, scale_ref, wout_ref, g_ref, b_ref, o_ref, st_ref,
                        ue_ref, s2_ref, s4_ref, s8_ref, *, tm, sub, tiles):
    t = pl.program_id(1)
    lo, base = PAD_ROWS, PAD_ROWS + CARRY_ROWS
    stages = (ue_ref, s2_ref, s4_ref, s8_ref)

    @pl.when(t == 0)
    def _():
        ue_ref[0:base, :] = jnp.zeros((base, D_MODEL), F32)
        for ref in stages[1:]:
            ref[0:lo, :] = jnp.zeros((lo, D_MODEL), F32)

    xs, us = [], []
    for first in range(0, tm, sub):
        xs.append(x_ref[first:first + sub, :])
        us.append(_dot(xs[-1].astype(BF16), win_ref[...]))
        ue_ref[base + first:base + first + sub, :] = us[-1]
    for j, first in enumerate(range(0, tm, sub)):
        r0, r1 = base + first, base + first + sub
        x, u = xs[j], us[j]
        pos = t * tm + first + lax.broadcasted_iota(jnp.int32, (sub, 1), 0)
        s0 = lo if first == 0 else r0
        pooled = []
        for gi, w in enumerate(POOL_WINDOWS):
            cols = slice(gi * POOL_GROUP_DIM, (gi + 1) * POOL_GROUP_DIM)
            n_stages = w.bit_length() - 1
            for level in range(n_stages - 1):
                src, dst, k = stages[level], stages[level + 1], 2 ** level
                dst[s0:r1, cols] = src[s0:r1, cols] + src[s0 - k:r1 - k, cols]
            src, k = stages[n_stages - 1], w // 2
            win_sum = src[r0:r1, cols] + src[r0 - k:r1 - k, cols]
            inv_count = 1.0 / jnp.minimum(pos + 1, w).astype(F32)
            pooled.append(win_sum * inv_count - u[:, cols])
        o_ref[first:first + sub, :] = _pool_tail(x, u, pooled, wgrp_ref, scale_ref, wout_ref, g_ref, b_ref)

    tail = ue_ref[base + tm - CARRY_ROWS:base + tm, :]

    @pl.when(t == tiles - 1)
    def _():
        st_ref[0] = tail

    ue_ref[lo:base, :] = tail


def _pool_prompt(x, batch, w_in, w_grp, scale, w_out, g, b):
    rows = x.shape[0]
    seq = rows // batch
    sub = ROW_TILE
    tm = POOL_SUB_TILES * sub
    assert seq % tm == 0
    tiles = seq // tm
    row_spec = pl.BlockSpec((tm, D_MODEL), lambda bi, t: (bi * tiles + t, 0))
    return pl.pallas_call(
        functools.partial(_pool_prompt_kernel, tm=tm, sub=sub, tiles=tiles), grid=(batch, tiles),
        in_specs=[row_spec, _resident(w_in.shape), _resident(w_grp.shape), _resident(scale.shape),
                  _resident(w_out.shape), _resident(g.shape), _resident(b.shape)],
        out_specs=[row_spec, pl.BlockSpec((1, CARRY_ROWS, D_MODEL), lambda bi, t: (bi, 0, 0))],
        out_shape=[jax.ShapeDtypeStruct((rows, D_MODEL), F32),
                   jax.ShapeDtypeStruct((batch, CARRY_ROWS, D_MODEL), F32)],
        scratch_shapes=[pltpu.VMEM((PAD_ROWS + CARRY_ROWS + tm, D_MODEL), F32)] * 4,
        compiler_params=pltpu.CompilerParams(dimension_semantics=("arbitrary", "arbitrary"),
                                             vmem_limit_bytes=VMEM_LIMIT_BYTES),
        name="pool_prompt")(x, w_in, w_grp, scale, w_out, g, b)


def _pool_sample_kernel(x_ref, st_ref, win_ref, wgrp_ref, scale_ref, wout_ref, g_ref, b_ref, o_ref, nst_ref,
                        *, steps, batch):
    x = x_ref[...]
    u = _dot(x.astype(BF16), win_ref[...])
    ext = [st_ref[k] for k in range(POOL_PAD)] + [u[i * batch:(i + 1) * batch, :] for i in range(steps)]
    pooled = []
    for gi, w in enumerate(POOL_WINDOWS):
        cols = slice(gi * POOL_GROUP_DIM, (gi + 1) * POOL_GROUP_DIM)
        per_step = []
        for i in range(steps):
            win_sum = ext[POOL_PAD + i][:, cols]
            for k in range(1, w):
                win_sum = win_sum + ext[POOL_PAD + i - k][:, cols]
            count = float(min(PAST_LEN + i + 1, w))
            per_step.append(win_sum / count - ext[POOL_PAD + i][:, cols])
        pooled.append(jnp.concatenate(per_step, axis=0))
    o_ref[...] = _pool_tail(x, u, pooled, wgrp_ref, scale_ref, wout_ref, g_ref, b_ref)
    for k in range(POOL_PAD):
        nst_ref[k] = ext[steps + k]


def _pool_sample(x, st, steps, batch, w_in, w_grp, scale, w_out, g, b):
    rows = x.shape[0]
    full = lambda shape: pl.BlockSpec(shape, lambda i: (0,) * len(shape))
    return pl.pallas_call(
        functools.partial(_pool_sample_kernel, steps=steps, batch=batch), grid=(1,),
        in_specs=[full(x.shape), full(st.shape), full(w_in.shape), full(w_grp.shape), full(scale.shape),
                  full(w_out.shape), full(g.shape), full(b.shape)],
        out_specs=[full(x.shape), full(st.shape)],
        out_shape=[jax.ShapeDtypeStruct((rows, D_MODEL), F32), jax.ShapeDtypeStruct(st.shape, F32)],
        compiler_params=pltpu.CompilerParams(dimension_semantics=("arbitrary",),
                                             vmem_limit_bytes=VMEM_LIMIT_BYTES),
        name="pool_sample")(x, st, w_in, w_grp, scale, w_out, g, b)


def _attn_prompt_kernel(*refs):
    n = N_DIL_GROUPS
    q_refs, k_refs, v_refs, kp_refs, vp_refs = (refs[i * n:(i + 1) * n] for i in range(5))
    bias_ref, o_ref = refs[5 * n:5 * n + 2]
    tail_refs = refs[5 * n + 2:6 * n + 2]
    oslab, mslab, lslab, btile = refs[6 * n + 2:]
    c = pl.program_id(2)
    key_col = lax.broadcasted_iota(jnp.int32, (1, 2 * Q_BLOCK), 1)

    @pl.when(c == 0)
    def _():
        for g in range(N_DIL_GROUPS):
            for hh in range(2):
                row = jnp.broadcast_to(bias_ref[g, 0, hh:hh + 1, :] * LOG2E, (Q_BLOCK, 2 * Q_BLOCK))
                tile = pltpu.roll(row, 0, 1, stride=1, stride_axis=0)
                btile[g, hh, 0] = tile
                btile[g, hh, 1] = jnp.where(key_col < Q_BLOCK, NEG, tile)

    lane = lax.broadcasted_iota(jnp.int32, (1, LANES), 1)
    first_head = lane < HEAD_DIM
    ones = jnp.ones((2 * Q_BLOCK, LANES), BF16)
    assert DIL_PATTERNS[0][1] == 1

    def attend(first_chunk):
        for g in reversed(range(N_DIL_GROUPS)):
            dil = DIL_PATTERNS[g][1]
            sub_blocks = ATTN_CHUNK // (Q_BLOCK * dil)
            span = Q_BLOCK * dil

            def rows(res, sb, dil=dil, span=span):
                start = res + sb * span
                return pl.ds(start, Q_BLOCK) if dil == 1 else pl.ds(start, Q_BLOCK, stride=dil)

            for res in range(dil):
                k_prev = kp_refs[g][0, rows(res, 0), :].astype(BF16)
                v_prev = vp_refs[g][0, rows(res, 0), :].astype(BF16)
                for sb in range(sub_blocks):
                    k_cur = k_refs[g][0, rows(res, sb), :].astype(BF16)
                    v_cur = v_refs[g][0, rows(res, sb), :].astype(BF16)
                    kk = jnp.concatenate([k_prev, k_cur], axis=0)
                    vv = jnp.concatenate([jnp.concatenate([v_prev, v_cur], axis=0), ones], axis=1)
                    qb = q_refs[g][0, rows(res, sb), :].astype(BF16)
                    no_prev = int(first_chunk and sb == 0)
                    outs, maxes, sums = [], [], []
                    for hh in range(2):
                        head_lanes = first_head if hh == 0 else jnp.logical_not(first_head)
                        qh = jnp.where(head_lanes, qb, jnp.zeros_like(qb))
                        s = lax.dot_general(qh, kk, (((1,), (1,)), ((), ())), preferred_element_type=F32)
                        s = s + btile[g, hh, no_prev]
                        m = jnp.max(s, axis=1, keepdims=True)
                        pv = _dot(jnp.exp2(s - m).astype(BF16), vv)
                        outs.append(pv[:, :LANES])
                        maxes.append(m)
                        sums.append(pv[:, LANES:])
                    out = jnp.where(first_head, outs[0], outs[1])
                    row_max = jnp.where(first_head, maxes[0], maxes[1])
                    row_sum = jnp.where(first_head, sums[0], sums[1])
                    k_prev, v_prev = k_cur, v_cur
                    if g > 0:
                        oslab[g - 1, rows(res, sb), :] = out
                        mslab[g - 1, rows(res, sb), :] = row_max
                        lslab[g - 1, rows(res, sb), :] = row_sum
                        continue
                    blk = rows(res, sb)
                    ms = [row_max] + [mslab[j, blk, :] for j in range(N_DIL_GROUPS - 1)]
                    top = functools.reduce(jnp.maximum, ms)
                    es = [jnp.exp2(m - top) for m in ms]
                    den, num = es[0] * row_sum, es[0] * out
                    for j in range(N_DIL_GROUPS - 1):
                        den = den + es[j + 1] * lslab[j, blk, :]
                        num = num + es[j + 1] * oslab[j, blk, :]
                    o_ref[0, blk, :] = (num / den).astype(o_ref.dtype)

    pl.when(c == 0)(lambda: attend(True))
    pl.when(c != 0)(lambda: attend(False))

    @pl.when(c == pl.num_programs(2) - 1)
    def _():
        for g, (win, _) in enumerate(DIL_PATTERNS):
            tail_refs[g][0, 0] = k_refs[g][0, ATTN_CHUNK - win:, :].T
            tail_refs[g][0, 1] = v_refs[g][0, ATTN_CHUNK - win:, :].T


def _group_bias(rel_bias, g):
    buckets = _bucket_table(np.arange(N_KEYS) * DIL_PATTERNS[g][1])
    return rel_bias[buckets][:, g * HEADS_PER_GROUP:(g + 1) * HEADS_PER_GROUP]


def _prompt_bias_rows(rel_bias):
    rows = []
    for g in range(N_DIL_GROUPS):
        by_key = _group_bias(rel_bias, g)[::-1].T
        rows.append(jnp.pad(by_key, ((0, 0), (0, 2 * Q_BLOCK - N_KEYS)), constant_values=NEG))
    return jnp.stack(rows).reshape(N_DIL_GROUPS, HEADS_PER_GROUP // 2, 2, 2 * Q_BLOCK).astype(F32)


def _attn_prompt(q, kv, batch, bias_rows):
    pairs = ATTN_GROUP_WIDTH // LANES
    seq = q.shape[1] // batch
    chunks = seq // ATTN_CHUNK
    blk = (1, ATTN_CHUNK, LANES)

    def spec(lane_block):
        return pl.BlockSpec(blk, lambda b, hp, c: (lane_block * pairs + hp, b * chunks + c, 0))

    def prev_spec(lane_block, dil):
        spans = ATTN_CHUNK // (Q_BLOCK * dil)
        return pl.BlockSpec((1, Q_BLOCK * dil, LANES),
                            lambda b, hp, c: (lane_block * pairs + hp,
                                              b * chunks * spans + jnp.maximum(c * spans - 1, 0), 0))

    groups = range(N_DIL_GROUPS)
    dils = [dil for _, dil in DIL_PATTERNS]
    in_specs = ([spec(g) for g in groups] + [spec(0)] * N_DIL_GROUPS + [spec(1)] * N_DIL_GROUPS
                + [prev_spec(0, dils[g]) for g in groups] + [prev_spec(1, dils[g]) for g in groups]
                + [pl.BlockSpec((N_DIL_GROUPS, 1, 2, 2 * Q_BLOCK), lambda b, hp, c: (0, hp, 0, 0))])
    return pl.pallas_call(
        _attn_prompt_kernel, grid=(batch, pairs, chunks), in_specs=in_specs,
        out_specs=[pl.BlockSpec(blk, lambda b, hp, c: (hp, b * chunks + c, 0))]
                  + [pl.BlockSpec((1, 2, LANES, win), lambda b, hp, c: (b, 0, hp, 0)) for win, _ in DIL_PATTERNS],
        out_shape=[jax.ShapeDtypeStruct((pairs, batch * seq, LANES), BF16)]
                  + [jax.ShapeDtypeStruct((batch, 2, ATTN_GROUP_WIDTH, win), F32) for win, _ in DIL_PATTERNS],
        scratch_shapes=[pltpu.VMEM((N_DIL_GROUPS - 1, ATTN_CHUNK, LANES), F32)] * 3
                       + [pltpu.VMEM((N_DIL_GROUPS, 2, 2, Q_BLOCK, 2 * Q_BLOCK), F32)],
        compiler_params=pltpu.CompilerParams(dimension_semantics=("arbitrary", "arbitrary", "arbitrary"),
                                             vmem_limit_bytes=VMEM_LIMIT_BYTES),
        name="attn_prompt")(*([q] * N_DIL_GROUPS + list(kv) * 4), bias_rows)


def _expand_heads(x):
    width = x.shape[-1]
    return jnp.broadcast_to(x[:, None, :], (HEADS_PER_GROUP, HEAD_DIM, width)).reshape(ATTN_GROUP_WIDTH, width)


def _head_sums(x):
    return x.reshape(HEADS_PER_GROUP, HEAD_DIM, x.shape[-1]).sum(axis=1)


def _softmax_parts(s):
    m = jnp.max(s, axis=1, keepdims=True)
    p = jnp.exp(s - m)
    return m, p, jnp.sum(p, axis=1, keepdims=True)


def _attn_sample_rows(q_ref, kv_ref, cache_refs, bias_refs, steps):
    c0_ref, c1_ref, c2_ref = cache_refs
    b0_ref, b1_ref, b2_ref = bias_refs
    lane = lax.broadcasted_iota(jnp.int32, (1, LANES), 1)
    sub = lax.broadcasted_iota(jnp.int32, (LANES, 1), 0)
    group_cols = KV_WIDTH // N_DIL_GROUPS

    def lanes_from_rows(mat, period):
        res = sub & (period - 1)
        rows = jnp.zeros((LANES, ATTN_GROUP_WIDTH), F32)
        for i in range(steps):
            rows = jnp.where(res == i, mat[i:i + 1, :], rows)
        return rows.T

    def new_rows(g, which, period):
        lo = g * group_cols + which * ATTN_GROUP_WIDTH
        return lanes_from_rows(kv_ref[0, :, lo:lo + ATTN_GROUP_WIDTH], period)

    outs = [[None] * N_DIL_GROUPS for _ in range(steps)]
    stats = [[None] * N_DIL_GROUPS for _ in range(steps)]

    q0 = q_ref[0, :, 0:ATTN_GROUP_WIDTH]
    kn, vn = new_rows(0, 0, LANES), new_rows(0, 1, LANES)
    for i in range(steps):
        qpat = jnp.broadcast_to(q0[i:i + 1, :], (LANES, ATTN_GROUP_WIDTH)).T
        s = jnp.concatenate([_head_sums(c0_ref[0, 0] * qpat), _head_sums(kn * qpat)], axis=1) + b0_ref[i]
        m, p, l = _softmax_parts(s)
        acc = c0_ref[0, 1] * _expand_heads(p[:, :LANES]) + vn * _expand_heads(p[:, LANES:])
        outs[i][0] = jnp.sum(acc, axis=1, keepdims=True)
        stats[i][0] = (m, l)

    for g, c_ref, b_ref in ((1, c1_ref, b1_ref), (2, c2_ref, b2_ref)):
        win, dil = DIL_PATTERNS[g]
        tiles = win // LANES
        qpat = lanes_from_rows(q_ref[0, :, g * ATTN_GROUP_WIDTH:(g + 1) * ATTN_GROUP_WIDTH], dil)
        kn, vn = new_rows(g, 0, dil), new_rows(g, 1, dil)
        parts = [_head_sums(c_ref[0, 0, :, t * LANES:(t + 1) * LANES] * qpat) for t in range(tiles)]
        s_all = jnp.concatenate(parts + [_head_sums(kn * qpat)], axis=1) + b_ref[...]
        res_all = lax.broadcasted_iota(jnp.int32, (1, win + LANES), 1) & (dil - 1)
        p_all = jnp.zeros_like(s_all)
        for i in range(steps):
            m, p, l = _softmax_parts(jnp.where(res_all == i, s_all, NEG))
            p_all = p_all + p
            stats[i][g] = (m, l)
        acc = vn * _expand_heads(p_all[:, win:])
        for t in range(tiles):
            acc = acc + c_ref[0, 1, :, t * LANES:(t + 1) * LANES] * _expand_heads(p_all[:, t * LANES:(t + 1) * LANES])
        res = lane & (dil - 1)
        for i in range(steps):
            outs[i][g] = jnp.sum(jnp.where(res == i, acc, 0.0), axis=1, keepdims=True)

    cols = jnp.zeros((ATTN_GROUP_WIDTH, LANES), F32)
    for i in range(steps):
        lses = [m + jnp.log(l) for m, l in stats[i]]
        top = jnp.maximum(jnp.maximum(lses[0], lses[1]), lses[2])
        es = [jnp.exp(x - top) for x in lses]
        den = es[0] + es[1] + es[2]
        col = sum(outs[i][g] * _expand_heads(es[g] / stats[i][g][1]) for g in range(N_DIL_GROUPS))
        cols = jnp.where(lane == i, col / _expand_heads(den), cols)
    return cols.T[0:steps, :]


def _sample_bias(rel_bias, steps):
    by_dist = [_group_bias(rel_bias, g).T for g in range(N_DIL_GROUPS)]
    win0 = DIL_PATTERNS[0][0]
    far_to_near = by_dist[0][:, ::-1]
    per_query = []
    for i in range(steps):
        cache = jnp.pad(far_to_near[:, :win0 - i], ((0, 0), (i, LANES - win0)), constant_values=NEG)
        new = jnp.pad(by_dist[0][:, :i + 1][:, ::-1], ((0, 0), (0, LANES - 1 - i)), constant_values=NEG)
        per_query.append(jnp.concatenate([cache, new], axis=1))
    tables = [jnp.stack(per_query).astype(F32)]
    for g in (1, 2):
        win, dil = DIL_PATTERNS[g]
        cache = jnp.repeat(by_dist[g][:, :0:-1], dil, axis=1)
        cache = jnp.where((np.arange(win) % dil < steps)[None], cache, NEG)
        new = jnp.pad(jnp.broadcast_to(by_dist[g][:, 0:1], (HEADS_PER_GROUP, steps)),
                      ((0, 0), (0, LANES - steps)), constant_values=NEG)
        tables.append(jnp.concatenate([cache, new], axis=1).astype(F32))
    return tables


def kernel(x_prompt, x_sample, state_pool, cache_kv_w128, cache_kv_w512, cache_kv_w2048, ln_g, ln_b, ffn_w13,
           ffn_w2, pool_w_in, pool_w_grp, pool_scale, pool_w_out, attn_w_kv, attn_w_q, attn_w_o, rel_bias):
    batch, seq, _ = x_prompt.shape
    dec_batch, steps, _ = x_sample.shape
    assert DEPTH == 2 and state_pool.shape[0] == 1 and attn_w_q.shape[0] == 1
    assert seq % ATTN_CHUNK == 0 and seq >= DIL_PATTERNS[-1][0]

    scale = pool_scale[0][None]
    ln = lambda layer, k: (ln_g[layer, k][None], ln_b[layer, k][None])
    q_scale = HEAD_DIM ** -0.5
    grp_rows = len(POOL_WINDOWS) * POOL_GROUP_DIM

    w13_00 = (ffn_w13, (0, 0))
    w2_00 = (ffn_w2, (0, 0))

    xp = x_prompt.reshape(batch * seq, D_MODEL)
    xs = jnp.swapaxes(x_sample, 0, 1).reshape(steps * dec_batch, D_MODEL)

    xp, xs, w13_01, w2_01, w_kv, w_in, w_grp, w_out = _ffn(
        xp, xs, w13_00, w2_00, *ln(0, 0),
        cast=((ffn_w13, (0, 1)), (ffn_w2, (0, 1)), (attn_w_kv, ()), (pool_w_in, (0,)),
              (pool_w_grp.reshape(1, grp_rows, POOL_GROUP_DIM), (0,)), (pool_w_out, (0,))))
    w_grp = w_grp.reshape(len(POOL_WINDOWS), POOL_GROUP_DIM, POOL_GROUP_DIM)
    xp, pool_p = _pool_prompt(xp, batch, w_in, w_grp, scale, w_out, *ln(0, 1))
    xs, pool_s = _pool_sample(xs, jnp.swapaxes(state_pool[0], 0, 1), steps, dec_batch,
                              w_in, w_grp, scale, w_out, *ln(0, 1))
    xp, xs, *kv_p, kv_s, w13_10, w2_10, w_q = _ffn(
        xp, xs, w13_01, w2_01, *ln(0, 2), wp=w_kv, proj_split=N_DIL_GROUPS,
        cast=((ffn_w13, (1, 0)), (ffn_w2, (1, 0)), (attn_w_q, (0,))))

    xp, xs, q_p, q_s, w13_11, w2_11, w_o = _ffn(
        xp, xs, w13_10, w2_10, *ln(1, 0), wp=w_q, proj_scales=(q_scale * LOG2E, q_scale),
        cast=((ffn_w13, (1, 1)), (ffn_w2, (1, 1)), (attn_w_o, (0,))))
    attn_p, *kv_tails = _attn_prompt(q_p, kv_p, batch, _prompt_bias_rows(rel_bias))
    kv_s = jnp.swapaxes(kv_s.reshape(steps, dec_batch, KV_WIDTH), 0, 1)
    q_s = jnp.swapaxes(q_s.reshape(steps, dec_batch, Q_WIDTH), 0, 1)
    y_prompt, y_sample = _oproj_ffn(
        xp, attn_p, xs, q_s, kv_s,
        (cache_kv_w128, cache_kv_w512, cache_kv_w2048), _sample_bias(rel_bias, steps),
        w_o, *ln(1, 1), w13_11, w2_11, *ln(1, 2))

    y_prompt = y_prompt.reshape(batch, seq, D_MODEL)
    y_sample = jnp.swapaxes(y_sample.reshape(steps, dec_batch, D_MODEL), 0, 1)
    pool_p = pool_p[:, CARRY_ROWS - POOL_PAD:][None]
    pool_s = jnp.swapaxes(pool_s, 0, 1)[None]
    kv_p_groups = [jnp.transpose(t.reshape(batch, 2, HEADS_PER_GROUP, HEAD_DIM, w), (0, 4, 1, 2, 3))
                   for t, (w, _) in zip(kv_tails, DIL_PATTERNS)]
    kv_s = kv_s.reshape(dec_batch, steps, N_DIL_GROUPS, 2, HEADS_PER_GROUP, HEAD_DIM)

    return (y_prompt, y_sample, pool_p, pool_s, kv_p_groups[0], kv_p_groups[1], kv_p_groups[2],
            kv_s[:, :, 0], kv_s[:, :, 1], kv_s[:, :, 2])
```
